```python
import jax, jax.numpy as jnp
from jax import lax
import numpy as np

D_MODEL = 1024
BATCH = 2
SEQ = 8192
DEPTH = 2
DEC_BATCH = 128
DEC_SEQ = 1
PAST_LEN = 16384
PAGE_SIZE = 128

F32 = jnp.float32
EPS = 1e-6
NEG_INF = -1e30
QBLK = 128
D_MIX = D_MODEL
GROUP_W = D_MIX // 4
H_A = 4
D_NOPE = 64
D_ROPE = 32
D_LAT = 128
D_VA = GROUP_W // H_A
ROPE_BASE = 10000.0
MLA_SCALE = (D_NOPE + D_ROPE) ** -0.5
H_B = 4
D_NSA = GROUP_W // H_B
BLK = 64
N_SEL = 16
WINDOW = 512
NSA_SCALE = D_NSA ** -0.5
FORCE_SCORE = 1e4
POOL_WINDOWS = (2, 4, 8, 16)
POOL_CH = GROUP_W // 4
POOL_PAD = 15
H_D = 4
CHUNK = 128
D_VD = GROUP_W // H_D
N_EXP = 32
TOP_K = 4
D_FF = D_MODEL
SWIGLU_LIMIT = 7.0
SWIGLU_ALPHA = 1.702
IN_SPLITS = (H_A * (D_NOPE + D_ROPE), D_LAT, D_ROPE,
             H_B * D_NSA, 2 * D_NSA, 2 * D_NSA, 2 * D_NSA, 3 * H_B,
             GROUP_W, 2 * GROUP_W)
N_IN = sum(IN_SPLITS)

kernel_name = 'hymba_mla_nsa_pool_gmlp_moe_step'


def rms(x, g):
    xf = x.astype(F32)
    y = (xf * lax.rsqrt(jnp.mean(xf * xf, axis=-1, keepdims=True) + EPS)).astype(x.dtype)
    return y * g


def masked_softmax(s, mask):
    s = jnp.where(mask, s, NEG_INF)
    m = jnp.max(s, axis=-1, keepdims=True)
    e = jnp.where(mask, jnp.exp(s - m), 0.0)
    return e / jnp.maximum(jnp.sum(e, axis=-1, keepdims=True), 1e-30)


def alibi_slopes():
    return jnp.asarray([2.0 ** (-8.0 * (i + 1) / H_B) for i in range(H_B)], F32)


def rope(x, pos):
    half = D_ROPE // 2
    freqs = ROPE_BASE ** (-jnp.arange(half, dtype=F32) / half)
    ang = pos.astype(F32)[:, None] * freqs[None, :]
    shape = (1, pos.shape[0]) + (1,) * (x.ndim - 3) + (half,)
    cos, sin = jnp.cos(ang).reshape(shape), jnp.sin(ang).reshape(shape)
    x1, x2 = x[..., :half].astype(F32), x[..., half:].astype(F32)
    return jnp.concatenate([x1 * cos - x2 * sin, x2 * cos + x1 * sin], -1).astype(x.dtype)


def split_in(z):
    idx = [int(v) for v in np.cumsum(IN_SPLITS)[:-1]]
    return jnp.split(z, idx, axis=-1)


def map_query_blocks(fn, qpos, *xs):
    n = qpos.shape[0] // QBLK
    def to_blocks(a):
        return jnp.moveaxis(a.reshape((a.shape[0], n, QBLK) + a.shape[2:]), 1, 0)
    out = lax.map(lambda args: fn(*args), (qpos.reshape(n, QBLK),) + tuple(to_blocks(a) for a in xs))
    out = jnp.moveaxis(out, 0, 1)
    return out.reshape((out.shape[0], n * QBLK) + out.shape[3:])


def mla_queries(zq, pos, w_uk, g_q):
    b, t, _ = zq.shape
    q = zq.reshape(b, t, H_A, D_NOPE + D_ROPE)
    q_lat = jnp.einsum('bthn,hnc->bthc', q[..., :D_NOPE], w_uk)
    q_rot = rope(q[..., D_NOPE:], pos)
    gain = jnp.concatenate([g_q, jnp.ones((H_A, D_ROPE), g_q.dtype)], -1)
    return rms(jnp.concatenate([q_lat, q_rot], -1), gain)


def mla_keys(c_lat, kr, g_k):
    gain = jnp.concatenate([g_k, jnp.ones((D_ROPE,), g_k.dtype)])
    return rms(jnp.concatenate([c_lat, kr], -1), gain)


def mla_attend(q_hat, qpos, k_hat, c_lat, kpos):
    s = jnp.einsum('bthd,bsd->bhts', q_hat, k_hat).astype(F32) * MLA_SCALE
    p = masked_softmax(s, (kpos[None, :] <= qpos[:, None])[None, None])
    return jnp.einsum('bhts,bsc->bthc', p.astype(c_lat.dtype), c_lat)


def mla_out(o_lat, w_uv):
    b, t = o_lat.shape[:2]
    return jnp.einsum('bthc,hcv->bthv', o_lat, w_uv).reshape(b, t, GROUP_W)


def nsa_queries(zq, zg, g_q):
    b, t, _ = zq.shape
    q = rms(zq.reshape(b, t, H_B, D_NSA), g_q)
    gates = jax.nn.sigmoid(zg.reshape(b, t, H_B, 3))
    return q, gates


def nsa_rows(zcmp, zsel, zwin, g_k_sel, g_k_win):
    sel = jnp.concatenate([rms(zsel[..., :D_NSA], g_k_sel), zsel[..., D_NSA:]], -1)
    win = jnp.concatenate([rms(zwin[..., :D_NSA], g_k_win), zwin[..., D_NSA:]], -1)
    return zcmp, sel, win


def nsa_blocks(cmp_rows, sel_rows, w_ck, w_cv, g_k_cmp):
    b, s, _ = cmp_rows.shape
    pad = (-s) % BLK
    nb = (s + pad) // BLK
    cmp_b = jnp.pad(cmp_rows, ((0, 0), (0, pad), (0, 0))).reshape(b, nb, BLK, 2 * D_NSA)
    sel_b = jnp.pad(sel_rows, ((0, 0), (0, pad), (0, 0))).reshape(b, nb, BLK, 2 * D_NSA)
    kc = rms(jnp.einsum('bjld,l->bjd', cmp_b[..., :D_NSA], w_ck), g_k_cmp)
    vc = jnp.einsum('bjld,l->bjd', cmp_b[..., D_NSA:], w_cv)
    return kc, vc, sel_b[..., :D_NSA], sel_b[..., D_NSA:]


def nsa_attend(q, qpos, kc, vc, ks, vs, kw, vw, kwpos, gates, slopes):
    b, tq = q.shape[:2]
    nb = kc.shape[1]
    sl = slopes[:, None, None]
    cur = qpos // BLK
    cpos = (jnp.arange(nb, dtype=jnp.int32) + 1) * BLK - 1
    s_c = jnp.einsum('bthd,bjd->bhtj', q, kc).astype(F32) * NSA_SCALE \
        - sl * (qpos[:, None] - cpos[None, :]).astype(F32)
    p_c = masked_softmax(s_c, (cpos[None, :] <= qpos[:, None])[None, None])
    o_c = jnp.einsum('bhtj,bjd->bthd', p_c.astype(vc.dtype), vc)
    jb = jnp.arange(nb, dtype=jnp.int32)[None, :]
    forced = (jb == 0) | (jb == cur[:, None])
    imp = jnp.where(forced, FORCE_SCORE, jnp.where(jb > cur[:, None], -1.0, jnp.sum(p_c, axis=1)))
    _, idx = lax.top_k(imp, min(N_SEL, nb))
    bi = jnp.arange(b)[:, None, None]
    n_k = idx.shape[-1] * BLK
    kg = ks[bi, idx].reshape(b, tq, n_k, D_NSA)
    vg = vs[bi, idx].reshape(b, tq, n_k, D_NSA)
    spos = (idx[..., None] * BLK + jnp.arange(BLK, dtype=jnp.int32)).reshape(b, tq, n_k)
    s_s = jnp.einsum('bthd,btmd->bhtm', q, kg).astype(F32) * NSA_SCALE \
        - slopes[None, :, None, None] * (qpos[None, :, None] - spos)[:, None].astype(F32)
    p_s = masked_softmax(s_s, (spos <= qpos[None, :, None])[:, None])
    o_s = jnp.einsum('bhtm,btmd->bthd', p_s.astype(vg.dtype), vg)
    dist_w = qpos[:, None] - kwpos[None, :]
    s_w = jnp.einsum('bthd,bsd->bhts', q, kw).astype(F32) * NSA_SCALE - sl * dist_w.astype(F32)
    mask_w = (dist_w >= 0) & (dist_w < WINDOW) & (kwpos[None, :] >= 0)
    p_w = masked_softmax(s_w, mask_w[None, None])
    o_w = jnp.einsum('bhts,bsd->bthd', p_w.astype(vw.dtype), vw)
    g = gates[..., None]
    return g[..., 0, :] * o_c + g[..., 1, :] * o_s + g[..., 2, :] * o_w


def pool_mix(x_ext, pos, w_pool, pool_scale):
    t = pos.shape[0]
    cs = jnp.cumsum(jnp.pad(x_ext.astype(F32), ((0, 0), (1, 0), (0, 0))), axis=1)
    end = cs[:, POOL_PAD + 1:]
    x_new = x_ext[:, POOL_PAD:].astype(F32)
    outs = []
    for g, w in enumerate(POOL_WINDOWS):
        ch = slice(g * POOL_CH, (g + 1) * POOL_CH)
        start = cs[:, POOL_PAD + 1 - w: POOL_PAD + 1 - w + t, ch]
        cnt = jnp.minimum(pos + 1, w).astype(F32)[None, :, None]
        d = ((end[..., ch] - start) / cnt - x_new[..., ch]).astype(x_ext.dtype)
        outs.append(d @ w_pool[g])
    return jnp.concatenate(outs, -1) * pool_scale


def gmlp_prep(zd, g_sgu):
    b, t, _ = zd.shape
    uv = jax.nn.gelu(zd)
    u, v = uv[..., :GROUP_W], uv[..., GROUP_W:]
    v = rms(v.reshape(b, t, H_D, D_VD), g_sgu.reshape(H_D, D_VD)).reshape(b, t, GROUP_W)
    return u, v


def gmlp_mix(u, v, w_s, b_s):
    b, t, _ = v.shape
    n_pad = (-t) % CHUNK
    nc = (t + n_pad) // CHUNK
    vc = jnp.pad(v, ((0, 0), (0, n_pad), (0, 0))).reshape(b, nc, CHUNK, H_D, D_VD)
    ws = w_s * jnp.tril(jnp.ones((CHUNK, CHUNK), w_s.dtype))
    mixed = jnp.einsum('hij,bnjhd->bnihd', ws, vc) + jnp.transpose(b_s)[None, None, :, :, None]
    return u * mixed.reshape(b, nc * CHUNK, GROUP_W)[:, :t]


def merge_groups(ys, g_out, w_out):
    o = jnp.concatenate(ys, -1)
    b, t, _ = o.shape
    o = rms(o.reshape(b, t, 4, GROUP_W), g_out.reshape(4, GROUP_W)).reshape(b, t, D_MIX)
    return o @ w_out


def mixer_prompt(h, p, slopes):
    b, t, _ = h.shape
    pos = jnp.arange(t, dtype=jnp.int32)
    zq_a, zc_a, zr_a, zq_b, zcmp, zsel, zwin, zg, zc, zd = split_in(h @ p['w_in'])
    q_hat = mla_queries(zq_a, pos, p['w_uk'], p['g_q_mla'])
    c_lat = rms(zc_a, p['g_kv'])
    kr = rope(zr_a, pos)
    k_hat = mla_keys(c_lat, kr, p['g_k_mla'])
    o_lat = map_query_blocks(lambda qp, q: mla_attend(q, qp, k_hat, c_lat, pos), pos, q_hat)
    y_a = mla_out(o_lat, p['w_uv'])
    q_b, gates = nsa_queries(zq_b, zg, p['g_q_nsa'])
    cmp_rows, sel_rows, win_rows = nsa_rows(zcmp, zsel, zwin, p['g_k_sel'], p['g_k_win'])
    kc, vc, ks, vs = nsa_blocks(cmp_rows, sel_rows, p['w_cmp_k'], p['w_cmp_v'], p['g_k_cmp'])
    win_pad = jnp.pad(win_rows, ((0, 0), (WINDOW, 0), (0, 0)))

    def nsa_block(qp, q, g):
        start = qp[0]
        kv = lax.dynamic_slice_in_dim(win_pad, start, WINDOW + QBLK, axis=1)
        kwpos = start - WINDOW + jnp.arange(WINDOW + QBLK, dtype=jnp.int32)
        return nsa_attend(q, qp, kc, vc, ks, vs, kv[..., :D_NSA], kv[..., D_NSA:], kwpos, g, slopes)

    y_b = map_query_blocks(nsa_block, pos, q_b, gates).reshape(b, t, GROUP_W)
    x_ext = jnp.concatenate([jnp.zeros((b, POOL_PAD, GROUP_W), zc.dtype), zc], 1)
    y_c = pool_mix(x_ext, pos, p['w_pool'], p['pool_scale'])
    u, v = gmlp_prep(zd, p['g_sgu'])
    y_d = gmlp_mix(u, v, p['w_s'], p['b_s'])
    out = merge_groups((y_a, y_b, y_c, y_d), p['g_out'], p['w_out'])
    w_keep = min(WINDOW, t)
    return out, (c_lat, kr, cmp_rows, sel_rows, win_rows[:, t - w_keep:], x_ext[:, -POOL_PAD:])


def mixer_sample(h, p, slopes, cache_c, cache_kr, cache_cmp, cache_sel, win_buf, pool_buf, page_table):
    b, t, _ = h.shape
    past = page_table.shape[1] * cache_c.shape[1]
    pos = past + jnp.arange(t, dtype=jnp.int32)
    kpos = jnp.arange(past + t, dtype=jnp.int32)

    def gather(pool):
        return pool[page_table].reshape(b, past, pool.shape[-1])

    zq_a, zc_a, zr_a, zq_b, zcmp, zsel, zwin, zg, zc, zd = split_in(h @ p['w_in'])
    q_hat = mla_queries(zq_a, pos, p['w_uk'], p['g_q_mla'])
    c_lat = rms(zc_a, p['g_kv'])
    kr = rope(zr_a, pos)
    c_all = jnp.concatenate([gather(cache_c), c_lat], 1)
    kr_all = jnp.concatenate([gather(cache_kr), kr], 1)
    o_lat = mla_attend(q_hat, pos, mla_keys(c_all, kr_all, p['g_k_mla']), c_all, kpos)
    y_a = mla_out(o_lat, p['w_uv'])
    q_b, gates = nsa_queries(zq_b, zg, p['g_q_nsa'])
    cmp_rows, sel_rows, win_rows = nsa_rows(zcmp, zsel, zwin, p['g_k_sel'], p['g_k_win'])
    kc, vc, ks, vs = nsa_blocks(jnp.concatenate([gather(cache_cmp), cmp_rows], 1),
                                jnp.concatenate([gather(cache_sel), sel_rows], 1),
                                p['w_cmp_k'], p['w_cmp_v'], p['g_k_cmp'])
    w_buf = win_buf.shape[1]
    win_all = jnp.concatenate([win_buf, win_rows], 1)
    kwpos = past - w_buf + jnp.arange(w_buf + t, dtype=jnp.int32)
    y_b = nsa_attend(q_b, pos, kc, vc, ks, vs, win_all[..., :D_NSA], win_all[..., D_NSA:],
                     kwpos, gates, slopes).reshape(b, t, GROUP_W)
    x_ext = jnp.concatenate([pool_buf, zc], 1)
    y_c = pool_mix(x_ext, pos, p['w_pool'], p['pool_scale'])
    u, v = gmlp_prep(zd, p['g_sgu'])
    y_d = gmlp_mix(u, v, p['w_s'], p['b_s'])
    out = merge_groups((y_a, y_b, y_c, y_d), p['g_out'], p['w_out'])
    return out, (c_lat, kr, cmp_rows, sel_rows, win_all[:, -w_buf:], x_ext[:, -POOL_PAD:], v)


def moe(h, p):
    b, t, d = h.shape
    x2 = h.reshape(b * t, d)
    logits = (x2 @ p['w_router'] + p['b_router']).astype(F32)
    top_v, top_i = lax.top_k(logits, TOP_K)
    wts = jax.nn.softmax(top_v, axis=-1)
    gates = jnp.sum(jax.nn.one_hot(top_i, N_EXP, dtype=F32) * wts[..., None], axis=1).astype(h.dtype)

    def expert(acc, e):
        wgu, bgu, wd, bd, g = e
        hu = x2 @ wgu + bgu
        gt = jnp.minimum(hu[:, :D_FF], SWIGLU_LIMIT)
        up = jnp.clip(hu[:, D_FF:], -SWIGLU_LIMIT, SWIGLU_LIMIT)
        act = gt * jax.nn.sigmoid(SWIGLU_ALPHA * gt) * (up + 1.0)
        return acc + g[:, None] * (act @ wd + bd), None

    y, _ = lax.scan(expert, jnp.zeros_like(x2), (p['w_gu'], p['b_gu'], p['w_down'], p['b_down'], gates.T))
    return y.reshape(b, t, d)


def residual_block(x, c, p, mixer):
    a = jax.nn.silu(c) @ p['w_ada'] + p['b_ada']
    sh1, sc1, g1, sh2, sc2, g2 = [z[:, None, :] for z in jnp.split(a, 6, axis=-1)]
    h = rms(x, p['norm1']) * (1.0 + sc1) + sh1
    mix, states = mixer(h)
    x = x + g1 * mix
    h2 = rms(x, p['norm2']) * (1.0 + sc2) + sh2
    x = x + g2 * moe(h2, p)
    return x, states


def _stack(lst, i):
    return jnp.stack([s[i] for s in lst], axis=0)


def setup_inputs(seed: int = 0) -> dict:
    key = jax.random.key(seed)
    keys = jax.random.split(key, 64)
    cnt = [0]

    def nk():
        cnt[0] += 1
        return keys[cnt[0] - 1]

    def nrm(shape, scale=1.0):
        return jax.random.normal(nk(), shape, F32) * scale

    def gain(shape):
        return 1.0 + 0.05 * nrm(shape)

    n_pages = PAST_LEN // PAGE_SIZE
    n_used = DEC_BATCH * n_pages
    n_pool = n_used + max(1, n_used // 4)
    w_buf = min(WINDOW, PAST_LEN)
    page_table = jax.random.permutation(nk(), n_pool)[:n_used].reshape(DEC_BATCH, n_pages).astype(jnp.int32)
    return {
        'x_prompt': nrm((BATCH, SEQ, D_MODEL)),
        'x_sample': nrm((DEC_BATCH, DEC_SEQ, D_MODEL)),
        'cache_mla_c': nrm((DEPTH, n_pool, PAGE_SIZE, D_LAT)),
        'cache_mla_kr': nrm((DEPTH, n_pool, PAGE_SIZE, D_ROPE)),
        'cache_nsa_cmp_kv': nrm((DEPTH, n_pool, PAGE_SIZE, 2 * D_NSA)),
        'cache_nsa_sel_kv': nrm((DEPTH, n_pool, PAGE_SIZE, 2 * D_NSA)),
        'cache_nsa_win_kv': nrm((DEPTH, DEC_BATCH, w_buf, 2 * D_NSA)),
        'state_pool': nrm((DEPTH, DEC_BATCH, POOL_PAD, GROUP_W)),
        'page_table': page_table,
        'c_prompt': nrm((BATCH, D_MODEL)),
        'c_sample': nrm((DEC_BATCH, D_MODEL)),
        'w_ada': nrm((DEPTH, D_MODEL, 6 * D_MODEL), 0.5 * D_MODEL ** -0.5),
        'b_ada': nrm((DEPTH, 6 * D_MODEL), 0.02),
        'norm1': gain((DEPTH, D_MODEL)),
        'norm2': gain((DEPTH, D_MODEL)),
        'w_in': nrm((DEPTH, D_MODEL, N_IN), D_MODEL ** -0.5),
        'w_uk': nrm((DEPTH, H_A, D_NOPE, D_LAT), D_NOPE ** -0.5),
        'w_uv': nrm((DEPTH, H_A, D_LAT, D_VA), D_LAT ** -0.5),
        'g_kv': gain((DEPTH, D_LAT)),
        'g_q_mla': gain((DEPTH, H_A, D_LAT)),
        'g_k_mla': gain((DEPTH, D_LAT)),
        'g_q_nsa': gain((DEPTH, D_NSA)),
        'g_k_cmp': gain((DEPTH, D_NSA)),
        'g_k_sel': gain((DEPTH, D_NSA)),
        'g_k_win': gain((DEPTH, D_NSA)),
        'w_cmp_k': (1.0 + 0.1 * nrm((DEPTH, BLK))) / BLK,
        'w_cmp_v': (1.0 + 0.1 * nrm((DEPTH, BLK))) / BLK,
        'w_pool': nrm((DEPTH, 4, POOL_CH, POOL_CH), POOL_CH ** -0.5),
        'pool_scale': gain((DEPTH, GROUP_W)),
        'g_sgu': gain((DEPTH, GROUP_W)),
        'w_s': nrm((DEPTH, H_D, CHUNK, CHUNK), 0.5 * CHUNK ** -0.5),
        'b_s': gain((DEPTH, H_D, CHUNK)),
        'g_out': gain((DEPTH, D_MIX)),
        'w_out': nrm((DEPTH, D_MIX, D_MODEL), D_MIX ** -0.5),
        'w_router': nrm((DEPTH, D_MODEL, N_EXP), D_MODEL ** -0.5),
        'b_router': nrm((DEPTH, N_EXP), 0.01),
        'w_gu': nrm((DEPTH, N_EXP, D_MODEL, 2 * D_FF), D_MODEL ** -0.5),
        'b_gu': nrm((DEPTH, N_EXP, 2 * D_FF), 0.01),
        'w_down': nrm((DEPTH, N_EXP, D_FF, D_MODEL), D_FF ** -0.5),
        'b_down': nrm((DEPTH, N_EXP, D_MODEL), 0.01),
    }


def reference(x_prompt, x_sample, cache_mla_c, cache_mla_kr, cache_nsa_cmp_kv, cache_nsa_sel_kv,
              cache_nsa_win_kv, state_pool, page_table, c_prompt, c_sample,
              w_ada, b_ada, norm1, norm2, w_in, w_uk, w_uv, g_kv, g_q_mla, g_k_mla,
              g_q_nsa, g_k_cmp, g_k_sel, g_k_win, w_cmp_k, w_cmp_v, w_pool, pool_scale,
              g_sgu, w_s, b_s, g_out, w_out, w_router, b_router, w_gu, b_gu, w_down, b_down):
    slopes = alibi_slopes()
    xp, xs = x_prompt, x_sample
    st_p, st_s = [], []
    for l in range(DEPTH):
        p = {'w_ada': w_ada[l], 'b_ada': b_ada[l], 'norm1': norm1[l], 'norm2': norm2[l],
             'w_in': w_in[l], 'w_uk': w_uk[l], 'w_uv': w_uv[l], 'g_kv': g_kv[l],
             'g_q_mla': g_q_mla[l], 'g_k_mla': g_k_mla[l], 'g_q_nsa': g_q_nsa[l],
             'g_k_cmp': g_k_cmp[l], 'g_k_sel': g_k_sel[l], 'g_k_win': g_k_win[l],
             'w_cmp_k': w_cmp_k[l], 'w_cmp_v': w_cmp_v[l], 'w_pool': w_pool[l],
             'pool_scale': pool_scale[l], 'g_sgu': g_sgu[l], 'w_s': w_s[l], 'b_s': b_s[l],
             'g_out': g_out[l], 'w_out': w_out[l], 'w_router': w_router[l], 'b_router': b_router[l],
             'w_gu': w_gu[l], 'b_gu': b_gu[l], 'w_down': w_down[l], 'b_down': b_down[l]}
        xp, sp = residual_block(xp, c_prompt, p, lambda h: mixer_prompt(h, p, slopes))
        xs, ss = residual_block(
            xs, c_sample, p,
            lambda h: mixer_sample(h, p, slopes, cache_mla_c[l], cache_mla_kr[l], cache_nsa_cmp_kv[l],
                                   cache_nsa_sel_kv[l], cache_nsa_win_kv[l], state_pool[l], page_table))
        st_p.append(sp)
        st_s.append(ss)
    return (xp, xs,
            _stack(st_p, 0), _stack(st_s, 0),
            _stack(st_p, 1), _stack(st_s, 1),
            _stack(st_p, 2), _stack(st_s, 2),
            _stack(st_p, 3), _stack(st_s, 3),
            _stack(st_p, 4), _stack(st_s, 4),
            _stack(st_p, 5), _stack(st_s, 5),
            _stack(st_s, 6))
```

```python
import functools

import numpy as np
import jax
import jax.numpy as jnp
from jax import lax
from jax.experimental import pallas as pl
from jax.experimental.pallas import tpu as pltpu

F32 = jnp.float32
BF16 = jnp.bfloat16
I32 = jnp.int32
HI = lax.Precision.HIGHEST

EPS = 1e-6
NEG_INF = -1e30
D_MODEL = 1024
GROUP_W = 256
H_A, D_NOPE, D_ROPE, D_LAT, D_VA = 4, 64, 32, 128, 64
MLA_DIM = D_LAT + D_ROPE
MLA_SCALE = (D_NOPE + D_ROPE) ** -0.5
ROPE_BASE = 10000.0
H_B, D_NSA, BLK, N_SEL, WINDOW = 4, 64, 64, 16, 512
NSA_SCALE = D_NSA ** -0.5
FORCE_SCORE = 1e4
POOL_WINDOWS = (2, 4, 8, 16)
POOL_PAD = 15
H_D, CHUNK, D_VD = 4, 128, 64
N_EXP, TOP_K, D_FF = 32, 4, 1024
SWIGLU_LIMIT, SWIGLU_ALPHA = 7.0, 1.702
ALIBI = tuple(2.0 ** (-8.0 * (i + 1) / H_B) for i in range(H_B))

LANE = 128
VMEM_LIMIT = 56 * 1024 * 1024

C_QN, C_QR, C_QRS, C_CA, C_KR, C_KRS, C_QB, C_CMP, C_SEL, C_WIN, C_ZG, C_ZC, C_ZD, N_PAD = (
    0, 256, 384, 512, 640, 768, 896, 1152, 1280, 1408, 1536, 1664, 1920, 2432)


def _in_cols():
    qa, ca, ra, qb = 0, 384, 512, 544
    cmp_, sel, win, zg, zc, zd, n_in = 800, 928, 1056, 1184, 1196, 1452, 1964
    per = D_NOPE + D_ROPE
    half = D_ROPE // 2
    cols = []
    for h in range(H_A):
        cols += list(range(qa + h * per, qa + h * per + D_NOPE))
    for h in range(H_A):
        cols += list(range(qa + h * per + D_NOPE, qa + (h + 1) * per))
    for h in range(H_A):
        b = qa + h * per + D_NOPE
        cols += list(range(b + half, b + D_ROPE)) + list(range(b, b + half))
    cols += list(range(ca, ca + D_LAT))
    cols += list(range(ra, ra + D_ROPE)) * 4
    cols += (list(range(ra + half, ra + D_ROPE)) + list(range(ra, ra + half))) * 4
    cols += list(range(qb, qb + 256))
    cols += list(range(cmp_, cmp_ + 128)) + list(range(sel, sel + 128)) + list(range(win, win + 128))
    cols += list(range(zg, zg + 12)) + [n_in] * (LANE - 12)
    cols += list(range(zc, zc + 256))
    cols += list(range(zd, zd + 512))
    assert len(cols) == N_PAD
    return np.asarray(cols, np.int32)


_IN_COLS = _in_cols()


def _cparams(sem):
    return pltpu.CompilerParams(dimension_semantics=sem, vmem_limit_bytes=VMEM_LIMIT)


def _lane_iota(shape):
    return lax.broadcasted_iota(I32, shape, len(shape) - 1)


def _group_rsqrt(x, width):
    n = x.shape[-1]
    sq = x * x
    if width == n:
        return lax.rsqrt(jnp.sum(sq, -1, keepdims=True) * (1.0 / width) + EPS)
    lane = _lane_iota((1, n))
    out = jnp.zeros_like(x)
    for g in range(n // width):
        m = (lane >= g * width) & (lane < (g + 1) * width)
        ss = jnp.sum(jnp.where(m, sq, 0.0), -1, keepdims=True)
        out = jnp.where(m, lax.rsqrt(ss * (1.0 / width) + EPS), out)
    return out


def _ada_kernel(c_ref, w_ref, b_ref, o_ref):
    c = c_ref[...]
    s = c * jax.nn.sigmoid(c)
    o_ref[0] = jnp.dot(s.astype(BF16), w_ref[0].astype(BF16), preferred_element_type=F32) + b_ref[0]


def _ada(c_all, w_ada, b_ada):
    depth, d, n = w_ada.shape
    rows = c_all.shape[0]
    tn = 1536
    return pl.pallas_call(
        _ada_kernel,
        out_shape=jax.ShapeDtypeStruct((depth, rows, n), F32),
        grid=(depth, n // tn),
        in_specs=[pl.BlockSpec((rows, d), lambda l, j: (0, 0)),
                  pl.BlockSpec((1, d, tn), lambda l, j: (l, 0, j)),
                  pl.BlockSpec((1, 1, tn), lambda l, j: (l, 0, j))],
        out_specs=pl.BlockSpec((1, rows, tn), lambda l, j: (l, 0, j)),
        compiler_params=_cparams(("arbitrary", "arbitrary")),
        name="ada",
    )(c_all, w_ada, b_ada.reshape(depth, 1, n))


def _gelu_tanh(x):
    return 0.5 * x * (1.0 + jnp.tanh(0.7978845608028654 * (x + 0.044715 * (x * x * x))))


def _inproj_kernel(sample, tm, x_ref, sh_ref, sc_ref, n1_ref, w_ref, cos_ref, sin_ref, wuk_ref, gq_ref,
                   gkv_ref, gk_ref, gqn_ref, gcmp_ref, gsel_ref, gwin_ref, wc_ref, wpool_ref, pscale_ref,
                   gsgu_ref, ws_ref, bs_ref, gout_ref, pool_ref,
                   qm_ref, kh_ref, cl_ref, cv_ref, kr_ref, qn_ref, gt_ref, cmp_ref, sel_ref, selb_ref,
                   win_ref, winb_ref, kvc_ref, zc_ref, v_ref, yc_ref, yd_ref, carry_ref):
    i = pl.program_id(1)
    x = x_ref[0]
    xn = x * lax.rsqrt(jnp.mean(x * x, -1, keepdims=True) + EPS) * n1_ref[...]
    h = xn * (1.0 + sc_ref[0]) + sh_ref[0]
    z = jnp.dot(h.astype(BF16), w_ref[...], preferred_element_type=F32)
    cos4, sin4 = cos_ref[...], sin_ref[...]
    lane = _lane_iota((1, LANE))

    qlat = jnp.dot(z[:, C_QN:C_QN + 256].astype(BF16), wuk_ref[...], preferred_element_type=F32)
    qrot = z[:, C_QR:C_QR + LANE] * cos4 + z[:, C_QRS:C_QRS + LANE] * sin4
    for hh in range(H_A):
        lat = qlat[:, hh * D_LAT:(hh + 1) * D_LAT]
        rot = jnp.where((lane >= hh * D_ROPE) & (lane < (hh + 1) * D_ROPE), qrot, 0.0)
        ss = jnp.sum(lat * lat, -1, keepdims=True) + jnp.sum(rot * rot, -1, keepdims=True)
        r = lax.rsqrt(ss * (1.0 / MLA_DIM) + EPS) * MLA_SCALE
        qm_ref[0, :, hh * 256:hh * 256 + D_LAT] = (lat * r * gq_ref[:, hh * D_LAT:(hh + 1) * D_LAT]).astype(BF16)
        qm_ref[0, :, hh * 256 + D_LAT:(hh + 1) * 256] = (rot * r).astype(BF16)

    ca = z[:, C_CA:C_CA + LANE]
    c_lat = ca * _group_rsqrt(ca, LANE) * gkv_ref[...]
    kr4 = z[:, C_KR:C_KR + LANE] * cos4 + z[:, C_KRS:C_KRS + LANE] * sin4
    cl_ref[0] = c_lat
    cv_ref[0] = c_lat.astype(BF16)
    kr_ref[0] = kr4[:, :D_ROPE]
    ssk = jnp.sum(c_lat * c_lat, -1, keepdims=True) + 0.25 * jnp.sum(kr4 * kr4, -1, keepdims=True)
    rk = lax.rsqrt(ssk * (1.0 / MLA_DIM) + EPS)
    kh_ref[0, :, :LANE] = (c_lat * rk * gk_ref[...]).astype(BF16)
    kh_ref[0, :, LANE:] = (kr4 * rk).astype(BF16)

    zqb = z[:, C_QB:C_QB + 256]
    qn_ref[0] = zqb * _group_rsqrt(zqb, D_NSA) * gqn_ref[...] * NSA_SCALE
    gt_ref[0] = jax.nn.sigmoid(z[:, C_ZG:C_ZG + LANE])
    cmp_rows = z[:, C_CMP:C_CMP + LANE]
    cmp_ref[0] = cmp_rows
    khalf = lane < D_NSA

    def norm_k(zz, g_ref):
        ss = jnp.sum(jnp.where(khalf, zz * zz, 0.0), -1, keepdims=True)
        return jnp.where(khalf, zz * lax.rsqrt(ss * (1.0 / D_NSA) + EPS) * g_ref[...], zz)

    sel_rows = norm_k(z[:, C_SEL:C_SEL + LANE], gsel_ref)
    win_rows = norm_k(z[:, C_WIN:C_WIN + LANE], gwin_ref)
    sel_ref[0] = sel_rows
    selb_ref[0] = sel_rows.astype(BF16)
    win_ref[0] = win_rows
    winb_ref[0] = win_rows.astype(BF16)

    zc = z[:, C_ZC:C_ZC + 256]
    zd = z[:, C_ZD:C_ZD + 512]
    uv = _gelu_tanh(zd)
    u = uv[:, :256]
    v = uv[:, 256:]
    v = v * _group_rsqrt(v, D_VD) * gsgu_ref[...]
    lane2 = _lane_iota((1, 256))
    wsel = [lane2 < 64, lane2 < 128, lane2 < 192]

    def pick(a2, a4, a8, a16):
        return jnp.where(wsel[0], a2, jnp.where(wsel[1], a4, jnp.where(wsel[2], a8, a16)))

    wlane = pick(2.0, 4.0, 8.0, 16.0)
    if not sample:
        nbk = tm // BLK
        ksum = jnp.sum(cmp_rows.reshape(nbk, BLK, LANE) * wc_ref[...][None], axis=1)
        kvc_ref[0] = norm_k(ksum, gcmp_ref)

        @pl.when(i == 0)
        def _():
            carry_ref[...] = jnp.zeros_like(carry_ref)

        ext = jnp.concatenate([carry_ref[...], zc], axis=0)
        a2 = ext + pltpu.roll(ext, 1, 0)
        a4 = a2 + pltpu.roll(a2, 2, 0)
        a8 = a4 + pltpu.roll(a4, 4, 0)
        a16 = a8 + pltpu.roll(a8, 8, 0)
        wsum = pick(a2, a4, a8, a16)[16:]
        carry_ref[...] = zc[tm - 16:]
        pos = (i * tm + lax.broadcasted_iota(I32, (tm, 1), 0)).astype(F32)
        cnt = jnp.minimum(pos + 1.0, wlane)
        zc_ref[0] = zc[tm - 16:]
        v_ref[0] = v[tm - 16:]
        for c in range(tm // CHUNK):
            vc = v[c * CHUNK:(c + 1) * CHUNK]
            vst = jnp.concatenate(
                [jnp.where((lane2 >= hh * D_VD) & (lane2 < (hh + 1) * D_VD), vc, 0.0) for hh in range(H_D)], axis=0)
            mixed = jnp.dot(ws_ref[...], vst.astype(BF16), preferred_element_type=F32) + bs_ref[...]
            yd = u[c * CHUNK:(c + 1) * CHUNK] * mixed
            yd_ref[0, c * CHUNK:(c + 1) * CHUNK, :] = (
                yd * _group_rsqrt(yd, GROUP_W) * gout_ref[:, 768:1024]).astype(BF16)
    else:
        kvc_ref[0] = jnp.zeros_like(kvc_ref[0])
        pb = pool_ref[...]
        sums = []
        for w in POOL_WINDOWS:
            sums.append(zc + jnp.sum(pb[:, POOL_PAD - (w - 1):, :], axis=1))
        wsum = pick(*sums)
        cnt = wlane
        zc_ref[0] = zc
        v_ref[0] = v
        mixed = ws_ref[...] * v + bs_ref[...]
        yd = u * mixed
        yd_ref[0] = (yd * _group_rsqrt(yd, GROUP_W) * gout_ref[:, 768:1024]).astype(BF16)
    d = wsum / cnt - zc
    yc = jnp.dot(d.astype(BF16), wpool_ref[...], preferred_element_type=F32) * pscale_ref[...]
    yc_ref[0] = (yc * _group_rsqrt(yc, GROUP_W) * gout_ref[:, 512:768]).astype(BF16)


def _inproj(x, sh1, sc1, lw, pos0, pool_buf=None):
    sample = pool_buf is not None
    nb, t, d = x.shape
    tm = t if sample else min(512, t)
    nt = t // tm
    half = D_ROPE // 2
    freqs = ROPE_BASE ** (-jnp.arange(half, dtype=F32) / half)
    pos = (jnp.full((t,), pos0, I32) if sample else pos0 + jnp.arange(t, dtype=I32)).astype(F32)
    ang = pos[:, None] * freqs[None, :]
    cos, sin = jnp.cos(ang), jnp.sin(ang)
    cos4 = jnp.tile(jnp.concatenate([cos, cos], -1), (1, 4))
    sin4 = jnp.tile(jnp.concatenate([-sin, sin], -1), (1, 4))
    if sample:
        ws = jnp.repeat(lw['w_s'][:, 0, 0], D_VD)[None, :]
        bs = jnp.repeat(lw['b_s'][:, 0], D_VD)[None, :]
        pool = pool_buf
        pool_spec = pl.BlockSpec(pool.shape, lambda b, i: (0, 0, 0))
        ws_spec = pl.BlockSpec((1, 256), lambda b, i: (0, 0))
        bs_spec = pl.BlockSpec((1, 256), lambda b, i: (0, 0))
        tail = tm
    else:
        ws, bs = lw['ws_cat'], lw['bs_full']
        pool = jnp.zeros((1, 8, LANE), F32)
        pool_spec = pl.BlockSpec(pool.shape, lambda b, i: (0, 0, 0))
        ws_spec = pl.BlockSpec((CHUNK, 4 * CHUNK), lambda b, i: (0, 0))
        bs_spec = pl.BlockSpec((CHUNK, 256), lambda b, i: (0, 0))
        tail = 16
    nbk = max(tm // BLK, 8)

    def row(w):
        return pl.BlockSpec((1, tm, w), lambda b, i: (b, i, 0))

    def const(shape):
        return pl.BlockSpec(shape, lambda b, i: (0,) * len(shape))

    def tailspec(w):
        return pl.BlockSpec((1, tail, w), lambda b, i: (b, 0, 0))

    mod = row(d) if sample else pl.BlockSpec((1, 1, d), lambda b, i: (b, 0, 0))
    in_specs = [row(d), mod, mod,
                const((1, d)), const((d, N_PAD)),
                pl.BlockSpec((tm, LANE), lambda b, i: (i, 0)), pl.BlockSpec((tm, LANE), lambda b, i: (i, 0)),
                const((256, 512)), const((1, 512)), const((1, LANE)), const((1, LANE)), const((1, 256)),
                const((1, LANE)), const((1, LANE)), const((1, LANE)), const((BLK, LANE)), const((256, 256)),
                const((1, 256)), const((1, 256)), ws_spec, bs_spec, const((1, d)), pool_spec]
    outs = [('qm', 1024, BF16), ('kh', 256, BF16), ('cl', LANE, F32), ('cv', LANE, BF16), ('kr', D_ROPE, F32),
            ('qn', 256, F32), ('gt', LANE, F32), ('cmp', LANE, F32), ('sel', LANE, F32), ('selb', LANE, BF16),
            ('win', LANE, F32), ('winb', LANE, BF16)]
    out_shape = [jax.ShapeDtypeStruct((nb, t, w), dt) for _, w, dt in outs]
    out_specs = [row(w) for _, w, _ in outs]
    out_shape += [jax.ShapeDtypeStruct((nb, nt * nbk, LANE), F32)]
    out_specs += [pl.BlockSpec((1, nbk, LANE), lambda b, i: (b, i, 0))]
    out_shape += [jax.ShapeDtypeStruct((nb, tail, 256), F32), jax.ShapeDtypeStruct((nb, tail, 256), F32)]
    out_specs += [tailspec(256), tailspec(256)]
    out_shape += [jax.ShapeDtypeStruct((nb, t, 256), BF16), jax.ShapeDtypeStruct((nb, t, 256), BF16)]
    out_specs += [row(256), row(256)]
    res = pl.pallas_call(
        functools.partial(_inproj_kernel, sample, tm),
        out_shape=out_shape,
        grid=(nb, nt),
        in_specs=in_specs,
        out_specs=out_specs,
        scratch_shapes=[pltpu.VMEM((16, 256), F32)],
        compiler_params=_cparams(("arbitrary", "arbitrary")),
        name="inproj_sample" if sample else "inproj_prompt",
    )(x, sh1, sc1, lw['norm1'], lw['w_pad'], cos4, sin4, lw['wuk_bd'], lw['gq_mla'], lw['g_kv'], lw['g_k_mla'],
      lw['gq_nsa'], lw['g_k_cmp'], lw['g_k_sel'], lw['g_k_win'], lw['wc'], lw['wpool_bd'], lw['pool_scale'],
      lw['g_sgu'], ws, bs, lw['g_out'], pool)
    names = [n for n, _, _ in outs] + ['kvc', 'zc_tail', 'v_tail', 'yc', 'yd']
    return dict(zip(names, res))


def _block_diag(blocks):
    n = len(blocks)
    r, c = blocks[0].shape
    out = jnp.zeros((n * r, n * c), blocks[0].dtype)
    for k, b in enumerate(blocks):
        out = out.at[k * r:(k + 1) * r, k * c:(k + 1) * c].set(b)
    return out


def _pad_lanes(v, n=LANE):
    return jnp.concatenate([v, jnp.ones((n - v.shape[0],), v.dtype)])[None, :]


def _layer_weights(l, w):
    lw = {}
    w_in = jnp.concatenate([w['w_in'][l], jnp.zeros((D_MODEL, 1), F32)], axis=1)
    lw['w_pad'] = jnp.take(w_in, _IN_COLS, axis=1).astype(BF16)
    lw['norm1'] = w['norm1'][l][None, :]
    lw['norm2'] = w['norm2'][l][None, :]
    lw['wuk_bd'] = _block_diag([w['w_uk'][l][h] for h in range(H_A)]).astype(BF16)
    lw['wuv_bd'] = _block_diag([w['w_uv'][l][h] for h in range(H_A)]).astype(BF16)
    lw['wuv_wide'] = jnp.concatenate([w['w_uv'][l][h] for h in range(H_A)], axis=1).astype(BF16)
    lw['gq_mla'] = w['g_q_mla'][l].reshape(1, H_A * D_LAT)
    lw['g_kv'] = w['g_kv'][l][None, :]
    lw['g_k_mla'] = w['g_k_mla'][l][None, :]
    lw['gq_nsa'] = jnp.tile(w['g_q_nsa'][l], H_B)[None, :]
    lw['g_k_cmp'] = _pad_lanes(w['g_k_cmp'][l])
    lw['g_k_sel'] = _pad_lanes(w['g_k_sel'][l])
    lw['g_k_win'] = _pad_lanes(w['g_k_win'][l])
    lw['wc'] = jnp.concatenate([jnp.tile(w['w_cmp_k'][l][:, None], (1, D_NSA)),
                                jnp.tile(w['w_cmp_v'][l][:, None], (1, D_NSA))], axis=1)
    lw['wpool_bd'] = _block_diag([w['w_pool'][l][g] for g in range(4)]).astype(BF16)
    lw['pool_scale'] = w['pool_scale'][l][None, :]
    lw['g_sgu'] = w['g_sgu'][l][None, :]
    tril = jnp.tril(jnp.ones((CHUNK, CHUNK), F32))
    lw['ws_cat'] = jnp.concatenate([w['w_s'][l][h] * tril for h in range(H_D)], axis=1).astype(BF16)
    lw['bs_full'] = jnp.repeat(jnp.transpose(w['b_s'][l]), D_VD, axis=1)
    lw['w_s'] = w['w_s'][l]
    lw['b_s'] = w['b_s'][l]
    lw['g_out'] = w['g_out'][l][None, :]
    lw['w_out'] = w['w_out'][l].astype(BF16)
    lw['w_router'] = jnp.pad(w['w_router'][l], ((0, 0), (0, LANE - N_EXP)))
    lw['b_router'] = jnp.concatenate([w['b_router'][l], jnp.full((LANE - N_EXP,), NEG_INF, F32)])[None, :]
    lw['moe'] = {k: w[k][l] for k in ('w_gu', 'b_gu', 'w_down', 'b_down')}
    return lw


def _causal_pairs(nq):
    qi = np.concatenate([np.full((q + 1,), q, np.int32) for q in range(nq)])
    ki = np.concatenate([np.arange(q + 1, dtype=np.int32) for q in range(nq)])
    return jnp.asarray(qi), jnp.asarray(ki)


def _mla_kernel(tq, qi_ref, ki_ref, q_ref, k_ref, v_ref, wuv_ref, gout_ref, o_ref, q4_ref, m_ref, l_ref, acc_ref):
    p_id = pl.program_id(1)
    qi, ki = qi_ref[p_id], ki_ref[p_id]

    @pl.when(ki == 0)
    def _():
        for hh in range(H_A):
            q4_ref[hh * tq:(hh + 1) * tq, :] = q_ref[0, :, hh * 256:(hh + 1) * 256]
        m_ref[...] = jnp.full_like(m_ref, NEG_INF)
        l_ref[...] = jnp.zeros_like(l_ref)
        acc_ref[...] = jnp.zeros_like(acc_ref)

    def update(masked):
        s = lax.dot_general(q4_ref[...], k_ref[0], (((1,), (1,)), ((), ())), preferred_element_type=F32)
        if masked:
            row = lax.broadcasted_iota(I32, (tq, tq), 0)
            col = lax.broadcasted_iota(I32, (tq, tq), 1)
            keep = jnp.concatenate([col <= row] * H_A, axis=0)
            s = jnp.where(keep, s, NEG_INF)
        m_prev = m_ref[...]
        m_new = jnp.maximum(m_prev, jnp.max(s, -1, keepdims=True))
        p = jnp.exp(s - m_new)
        alpha = jnp.exp(m_prev - m_new)
        l_ref[...] = alpha * l_ref[...] + jnp.sum(p, -1, keepdims=True)
        acc_ref[...] = alpha * acc_ref[...] + jnp.dot(p.astype(BF16), v_ref[0], preferred_element_type=F32)
        m_ref[...] = m_new

    @pl.when(ki < qi)
    def _():
        update(False)

    @pl.when(ki == qi)
    def _():
        update(True)
        o = acc_ref[...] / l_ref[...]
        ocat = jnp.concatenate([o[hh * tq:(hh + 1) * tq] for hh in range(H_A)], axis=1)
        ya = jnp.dot(ocat.astype(BF16), wuv_ref[...], preferred_element_type=F32)
        o_ref[0] = (ya * _group_rsqrt(ya, GROUP_W) * gout_ref[:, 0:256]).astype(BF16)


def _mla_prompt(qm, kh, cv, lw):
    nb, t, _ = qm.shape
    tq = min(512, t)
    nq = t // tq
    qi, ki = _causal_pairs(nq)
    grid_spec = pltpu.PrefetchScalarGridSpec(
        num_scalar_prefetch=2,
        grid=(nb, qi.shape[0]),
        in_specs=[pl.BlockSpec((1, tq, 1024), lambda b, p, qi, ki: (b, qi[p], 0)),
                  pl.BlockSpec((1, tq, 256), lambda b, p, qi, ki: (b, ki[p], 0)),
                  pl.BlockSpec((1, tq, LANE), lambda b, p, qi, ki: (b, ki[p], 0)),
                  pl.BlockSpec((512, 256), lambda b, p, qi, ki: (0, 0)),
                  pl.BlockSpec((1, D_MODEL), lambda b, p, qi, ki: (0, 0))],
        out_specs=pl.BlockSpec((1, tq, 256), lambda b, p, qi, ki: (b, qi[p], 0)),
        scratch_shapes=[pltpu.VMEM((H_A * tq, 256), BF16), pltpu.VMEM((H_A * tq, 1), F32),
                        pltpu.VMEM((H_A * tq, 1), F32), pltpu.VMEM((H_A * tq, LANE), F32)])
    return pl.pallas_call(
        functools.partial(_mla_kernel, tq),
        out_shape=jax.ShapeDtypeStruct((nb, t, 256), BF16),
        grid_spec=grid_spec,
        compiler_params=_cparams(("arbitrary", "arbitrary")),
        name="mla_prompt",
    )(qi, ki, qm, kh, cv, lw['wuv_bd'], lw['g_out'])


def _topk_mask(imp, jb, n_blocks, k):
    sel = jnp.zeros(imp.shape, F32)
    for _ in range(k):
        m = jnp.max(imp, -1, keepdims=True)
        idx = jnp.min(jnp.where(imp == m, jb, n_blocks), -1, keepdims=True)
        pick = jb == idx
        sel = jnp.where(pick, 1.0, sel)
        imp = jnp.where(pick, -3e38, imp)
    return sel


def _masked_softmax(s, keep):
    s = jnp.where(keep, s, NEG_INF)
    m = jnp.max(s, -1, keepdims=True)
    e = jnp.where(keep, jnp.exp(s - m), 0.0)
    return e / jnp.maximum(jnp.sum(e, -1, keepdims=True), 1e-30)


def _online_update(s, keep, kv, m_ref, l_ref, acc_ref):
    s = jnp.where(keep, s, NEG_INF)
    m_prev = m_ref[...]
    m_new = jnp.maximum(m_prev, jnp.max(s, -1, keepdims=True))
    p = jnp.where(keep, jnp.exp(s - m_new), 0.0)
    alpha = jnp.exp(m_prev - m_new)
    l_ref[...] = alpha * l_ref[...] + jnp.sum(p, -1, keepdims=True)
    acc_ref[...] = alpha * acc_ref[...] + jnp.dot(p.astype(BF16), kv, preferred_element_type=F32)
    m_ref[...] = m_new


def _merge_heads(parts):
    lane = _lane_iota((1, LANE))
    lo = lane < D_NSA
    pair = [jnp.where(lo, pltpu.roll(parts[2 * j], D_NSA, 1), parts[2 * j + 1]) for j in range(2)]
    return jnp.concatenate(pair, axis=1)


def _head_cols(g, offset):
    lane2 = _lane_iota((1, 256))
    out = jnp.zeros((g.shape[0], 256), F32)
    for hh in range(H_B):
        col = g[:, 3 * hh + offset:3 * hh + offset + 1]
        out = jnp.where((lane2 >= hh * D_NSA) & (lane2 < (hh + 1) * D_NSA), col, out)
    return out


def _nsa_kernel(tq, n_blocks, wt, qi_ref, ki_ref, qn_ref, gt_ref, kvc_ref, selb_ref, winb_ref, gout_ref, o_ref,
                q4_ref, selm_ref, oc_ref, ms_ref, ls_ref, as_ref, mw_ref, lw_ref, aw_ref):
    p_id = pl.program_id(1)
    qi, ki = qi_ref[p_id], ki_ref[p_id]
    qpos = qi * tq + lax.broadcasted_iota(I32, (tq, 1), 0)

    @pl.when(ki == 0)
    def _():
        q = qn_ref[0]
        kvc = kvc_ref[0]
        kc = kvc[:, :D_NSA]
        vc = kvc[:, D_NSA:].astype(BF16)
        jb = lax.broadcasted_iota(I32, (1, n_blocks), 1)
        cpos = (jb + 1) * BLK - 1
        valid = cpos <= qpos
        dist = (qpos - cpos).astype(F32)
        imp = jnp.zeros((tq, n_blocks), F32)
        for hh in range(H_B):
            qh = q[:, hh * D_NSA:(hh + 1) * D_NSA]
            s = lax.dot_general(qh, kc, (((1,), (1,)), ((), ())), precision=HI, preferred_element_type=F32)
            pc = _masked_softmax(s - ALIBI[hh] * dist, valid)
            oc_ref[:, hh * D_NSA:(hh + 1) * D_NSA] = jnp.dot(pc.astype(BF16), vc, preferred_element_type=F32)
            imp = imp + pc
            q4_ref[hh * tq:(hh + 1) * tq, :] = jnp.concatenate(
                [qh, jnp.zeros((tq, LANE - D_NSA), F32)], axis=1).astype(BF16)
        cur = qpos // BLK
        imp = jnp.where((jb == 0) | (jb == cur), FORCE_SCORE, jnp.where(jb > cur, -1.0, imp))
        selm_ref[...] = _topk_mask(imp, jb, n_blocks, min(N_SEL, n_blocks)).astype(BF16)
        for m_r, l_r, a_r in ((ms_ref, ls_ref, as_ref), (mw_ref, lw_ref, aw_ref)):
            m_r[...] = jnp.full_like(m_r, NEG_INF)
            l_r[...] = jnp.zeros_like(l_r)
            a_r[...] = jnp.zeros_like(a_r)

    kpos = ki * tq + lax.broadcasted_iota(I32, (1, tq), 1)
    disti = qpos - kpos
    dist = disti.astype(F32)
    bias = jnp.concatenate([ALIBI[hh] * dist for hh in range(H_B)], axis=0)

    expand = (lax.broadcasted_iota(I32, (n_blocks, tq), 0) ==
              (ki * tq + lax.broadcasted_iota(I32, (n_blocks, tq), 1)) // BLK).astype(BF16)
    chosen = jnp.dot(selm_ref[...], expand, preferred_element_type=F32) > 0.5
    keep = chosen & (disti >= 0)
    kv = selb_ref[0]
    s = lax.dot_general(q4_ref[...], kv, (((1,), (1,)), ((), ())), preferred_element_type=F32) - bias
    _online_update(s, jnp.concatenate([keep] * H_B, axis=0), kv, ms_ref, ls_ref, as_ref)

    @pl.when(ki >= qi - wt)
    def _():
        keepw = (disti >= 0) & (disti < WINDOW)
        kvw = winb_ref[0]
        sw = lax.dot_general(q4_ref[...], kvw, (((1,), (1,)), ((), ())), preferred_element_type=F32) - bias
        _online_update(sw, jnp.concatenate([keepw] * H_B, axis=0), kvw, mw_ref, lw_ref, aw_ref)

    @pl.when(ki == qi)
    def _():
        o_s = as_ref[...] / jnp.maximum(ls_ref[...], 1e-30)
        o_w = aw_ref[...] / jnp.maximum(lw_ref[...], 1e-30)
        g = gt_ref[0]
        ys = _merge_heads([o_s[hh * tq:(hh + 1) * tq] for hh in range(H_B)])
        yw = _merge_heads([o_w[hh * tq:(hh + 1) * tq] for hh in range(H_B)])
        y = _head_cols(g, 0) * oc_ref[...] + _head_cols(g, 1) * ys + _head_cols(g, 2) * yw
        o_ref[0] = (y * _group_rsqrt(y, GROUP_W) * gout_ref[:, 256:512]).astype(BF16)


def _nsa_prompt(qn, gt, kvc, selb, winb, lw):
    nb, t, _ = qn.shape
    tq = min(512, t)
    nq = t // tq
    n_blocks = t // BLK
    wt = WINDOW // tq
    qi, ki = _causal_pairs(nq)

    def qmap(b, p, qi, ki):
        return (b, qi[p], 0)

    def kmap(b, p, qi, ki):
        return (b, ki[p], 0)

    def wmap(b, p, qi, ki):
        return (b, jnp.maximum(ki[p], qi[p] - wt), 0)

    grid_spec = pltpu.PrefetchScalarGridSpec(
        num_scalar_prefetch=2,
        grid=(nb, qi.shape[0]),
        in_specs=[pl.BlockSpec((1, tq, 256), qmap), pl.BlockSpec((1, tq, LANE), qmap),
                  pl.BlockSpec((1, n_blocks, LANE), lambda b, p, qi, ki: (b, 0, 0)),
                  pl.BlockSpec((1, tq, LANE), kmap), pl.BlockSpec((1, tq, LANE), wmap),
                  pl.BlockSpec((1, D_MODEL), lambda b, p, qi, ki: (0, 0))],
        out_specs=pl.BlockSpec((1, tq, 256), qmap),
        scratch_shapes=[pltpu.VMEM((H_B * tq, LANE), BF16), pltpu.VMEM((tq, n_blocks), BF16),
                        pltpu.VMEM((tq, 256), F32),
                        pltpu.VMEM((H_B * tq, 1), F32), pltpu.VMEM((H_B * tq, 1), F32),
                        pltpu.VMEM((H_B * tq, LANE), F32),
                        pltpu.VMEM((H_B * tq, 1), F32), pltpu.VMEM((H_B * tq, 1), F32),
                        pltpu.VMEM((H_B * tq, LANE), F32)])
    return pl.pallas_call(
        functools.partial(_nsa_kernel, tq, n_blocks, wt),
        out_shape=jax.ShapeDtypeStruct((nb, t, 256), BF16),
        grid_spec=grid_spec,
        compiler_params=_cparams(("arbitrary", "arbitrary")),
        name="nsa_prompt",
    )(qi, ki, qn, gt, kvc, selb, winb, lw['g_out'])


def _outproj_kernel(ya_ref, yb_ref, yc_ref, yd_ref, x_ref, g1_ref, sc_ref, sh_ref, n2_ref, wo_ref, wr_ref, br_ref,
                    x1_ref, h2_ref, ti_ref, tw_ref):
    mix = jnp.zeros(x_ref.shape, F32)
    for g, y_ref in enumerate((ya_ref, yb_ref, yc_ref, yd_ref)):
        mix = mix + jnp.dot(y_ref[...].astype(BF16), wo_ref[g * GROUP_W:(g + 1) * GROUP_W, :],
                            preferred_element_type=F32)
    x1 = x_ref[...] + g1_ref[0] * mix
    x1_ref[...] = x1
    h2 = x1 * lax.rsqrt(jnp.mean(x1 * x1, -1, keepdims=True) + EPS) * n2_ref[...]
    h2 = h2 * (1.0 + sc_ref[0]) + sh_ref[0]
    h2_ref[...] = h2
    logits = jnp.dot(h2, wr_ref[...], precision=HI, preferred_element_type=F32) + br_ref[...]
    lane = _lane_iota((1, LANE))
    ti = jnp.zeros(logits.shape, I32)
    tv = jnp.full(logits.shape, NEG_INF, F32)
    for k in range(TOP_K):
        m = jnp.max(logits, -1, keepdims=True)
        idx = jnp.min(jnp.where(logits == m, lane, LANE), -1, keepdims=True)
        ti = jnp.where(lane == k, idx, ti)
        tv = jnp.where(lane == k, m, tv)
        logits = jnp.where(lane == idx, -3e38, logits)
    e = jnp.where(lane < TOP_K, jnp.exp(tv - jnp.max(tv, -1, keepdims=True)), 0.0)
    ti_ref[...] = ti
    tw_ref[...] = e / jnp.sum(e, -1, keepdims=True)


def _outproj(ys, x, g1, sc2, sh2, lw, rows_per_mod):
    n, d = x.shape
    tm = min(512, n)
    per_row = rows_per_mod == 1
    if per_row:
        mod = pl.BlockSpec((1, tm, d), lambda i: (0, i, 0))
        g1, sc2, sh2 = [a.reshape(1, n, d) for a in (g1, sc2, sh2)]
    else:
        mod = pl.BlockSpec((1, 1, d), lambda i: (i * tm // rows_per_mod, 0, 0))
        g1, sc2, sh2 = [a.reshape(-1, 1, d) for a in (g1, sc2, sh2)]

    def row(w):
        return pl.BlockSpec((tm, w), lambda i: (i, 0))

    def const(shape):
        return pl.BlockSpec(shape, lambda i: (0, 0))

    return pl.pallas_call(
        _outproj_kernel,
        out_shape=[jax.ShapeDtypeStruct((n, d), F32), jax.ShapeDtypeStruct((n, d), F32),
                   jax.ShapeDtypeStruct((n, LANE), I32), jax.ShapeDtypeStruct((n, LANE), F32)],
        grid=(n // tm,),
        in_specs=[row(256)] * 4 + [row(d), mod, mod, mod, const((1, d)), const((d, d)), const((d, LANE)),
                                    const((1, LANE))],
        out_specs=[row(d), row(d), row(LANE), row(LANE)],
        compiler_params=_cparams(("arbitrary",)),
        name="outproj",
    )(*ys, x, g1, sc2, sh2, lw['norm2'], lw['w_out'], lw['w_router'], lw['b_router'])


GATHER_CHUNK = 2048


def _gather_kernel(idx_ref, src_ref, dst_ref, sem):
    i = pl.program_id(0)

    def body(j, carry):
        r = idx_ref[0, 0, j]
        pltpu.make_async_copy(src_ref.at[pl.ds(r, 1)], dst_ref.at[pl.ds(i * GATHER_CHUNK + j, 1)], sem).start()
        return carry

    lax.fori_loop(0, GATHER_CHUNK, body, 0)
    pltpu.make_async_copy(src_ref.at[pl.ds(0, GATHER_CHUNK)], dst_ref.at[pl.ds(i * GATHER_CHUNK, GATHER_CHUNK)],
                          sem).wait()


def _row_gather(src, idx):
    m = idx.shape[0]
    assert m % GATHER_CHUNK == 0
    nchunk = m // GATHER_CHUNK
    return pl.pallas_call(
        _gather_kernel,
        out_shape=jax.ShapeDtypeStruct((m, src.shape[1]), src.dtype),
        grid=(nchunk,),
        in_specs=[pl.BlockSpec((1, 1, GATHER_CHUNK), lambda i: (i, 0, 0), memory_space=pltpu.SMEM),
                  pl.BlockSpec(memory_space=pl.ANY)],
        out_specs=pl.BlockSpec(memory_space=pl.ANY),
        scratch_shapes=[pltpu.SemaphoreType.DMA],
        compiler_params=_cparams(("arbitrary",)),
        name="row_gather",
    )(idx.reshape(nchunk, 1, GATHER_CHUNK), src)


MOE_TM = 512


def _moe_kernel(te_ref, tv_ref, xs_ref, wgu_ref, bgu_ref, wd_ref, bd_ref, o_ref, wgu_bf, wd_bf):
    i = pl.program_id(0)
    prev = te_ref[jnp.maximum(i - 1, 0)]

    @pl.when((i == 0) | (te_ref[i] != prev))
    def _():
        for c in range(8):
            wgu_bf[c * 128:(c + 1) * 128, :] = wgu_ref[0, c * 128:(c + 1) * 128, :].astype(BF16)
            wd_bf[c * 128:(c + 1) * 128, :] = wd_ref[0, c * 128:(c + 1) * 128, :].astype(BF16)

    @pl.when(tv_ref[i] == 1)
    def _():
        x = xs_ref[...].astype(BF16)
        hu = jnp.dot(x, wgu_bf[...], preferred_element_type=F32) + bgu_ref[0]
        gt = jnp.minimum(hu[:, :D_FF], SWIGLU_LIMIT)
        up = jnp.clip(hu[:, D_FF:], -SWIGLU_LIMIT, SWIGLU_LIMIT)
        act = gt * jax.nn.sigmoid(SWIGLU_ALPHA * gt) * (up + 1.0)
        o_ref[...] = jnp.dot(act.astype(BF16), wd_bf[...], preferred_element_type=F32) + bd_ref[0]

    @pl.when(tv_ref[i] == 0)
    def _():
        o_ref[...] = jnp.zeros_like(o_ref)


def _moe_ffn(xs, tile_expert, tile_valid, w_gu, b_gu, w_down, b_down):
    r, d = xs.shape
    n_tiles = r // MOE_TM
    grid_spec = pltpu.PrefetchScalarGridSpec(
        num_scalar_prefetch=2,
        grid=(n_tiles,),
        in_specs=[pl.BlockSpec((MOE_TM, d), lambda i, te, tv: (i, 0)),
                  pl.BlockSpec((1, d, 2 * D_FF), lambda i, te, tv: (te[i], 0, 0)),
                  pl.BlockSpec((1, 1, 2 * D_FF), lambda i, te, tv: (te[i], 0, 0)),
                  pl.BlockSpec((1, D_FF, d), lambda i, te, tv: (te[i], 0, 0)),
                  pl.BlockSpec((1, 1, d), lambda i, te, tv: (te[i], 0, 0))],
        out_specs=pl.BlockSpec((MOE_TM, d), lambda i, te, tv: (i, 0)),
        scratch_shapes=[pltpu.VMEM((d, 2 * D_FF), BF16), pltpu.VMEM((D_FF, d), BF16)])
    return pl.pallas_call(
        _moe_kernel,
        out_shape=jax.ShapeDtypeStruct((r, d), F32),
        grid_spec=grid_spec,
        compiler_params=_cparams(("arbitrary",)),
        name="moe_ffn",
    )(tile_expert, tile_valid, xs, w_gu, b_gu.reshape(N_EXP, 1, 2 * D_FF), w_down, b_down.reshape(N_EXP, 1, d))


def _combine_kernel(x1_ref, y0_ref, y1_ref, y2_ref, y3_ref, tw_ref, g2_ref, o_ref):
    tw = tw_ref[...]
    acc = jnp.zeros(x1_ref.shape, F32)
    for k, y_ref in enumerate((y0_ref, y1_ref, y2_ref, y3_ref)):
        acc = acc + tw[:, k:k + 1] * y_ref[0]
    o_ref[...] = x1_ref[...] + g2_ref[0] * acc


def _combine(x1, y4, tw, g2, row_offset, rows_per_mod):
    n, d = x1.shape
    tm = min(512, n)
    off = row_offset // tm
    per_row = rows_per_mod == 1
    if per_row:
        mod = pl.BlockSpec((1, tm, d), lambda i: (0, i, 0))
        g2 = g2.reshape(1, n, d)
    else:
        mod = pl.BlockSpec((1, 1, d), lambda i: (i * tm // rows_per_mod, 0, 0))
        g2 = g2.reshape(-1, 1, d)
    yspecs = [pl.BlockSpec((1, tm, d), functools.partial(lambda k, i: (k, i + off, 0), k)) for k in range(TOP_K)]
    return pl.pallas_call(
        _combine_kernel,
        out_shape=jax.ShapeDtypeStruct((n, d), F32),
        grid=(n // tm,),
        in_specs=[pl.BlockSpec((tm, d), lambda i: (i, 0))] + yspecs +
                 [pl.BlockSpec((tm, LANE), lambda i: (i + off, 0)), mod],
        out_specs=pl.BlockSpec((tm, d), lambda i: (i, 0)),
        compiler_params=_cparams(("arbitrary",)),
        name="combine",
    )(x1, y4, y4, y4, y4, tw, g2)


def _moe(h2_all, ti_all, lw_moe):
    n, d = h2_all.shape
    pairs = n * TOP_K
    e = ti_all[:, :TOP_K].reshape(pairs)
    onehot = (e[:, None] == jnp.arange(N_EXP, dtype=I32)[None, :]).astype(I32)
    csum = jnp.cumsum(onehot, axis=0)
    rank = jnp.take_along_axis(csum, e[:, None], axis=1)[:, 0] - 1
    counts = csum[-1]
    padded = (counts + MOE_TM - 1) // MOE_TM * MOE_TM
    ends = jnp.cumsum(padded)
    starts = ends - padded
    row = starts[e] + rank
    r_max = -(-(pairs + N_EXP * MOE_TM) // GATHER_CHUNK) * GATHER_CHUNK
    tok_of_row = jnp.zeros((r_max,), I32).at[row].set(jnp.arange(pairs, dtype=I32) // TOP_K)
    tile_start = jnp.arange(r_max // MOE_TM, dtype=I32) * MOE_TM
    tile_valid = (tile_start < ends[-1]).astype(I32)
    tile_expert = jnp.minimum(jnp.searchsorted(ends, tile_start, side='right'), N_EXP - 1).astype(I32)
    tile_expert = jnp.where(tile_valid == 1, tile_expert, jnp.max(jnp.where(counts > 0, jnp.arange(N_EXP), 0)))
    xs = _row_gather(h2_all, tok_of_row)
    ys = _moe_ffn(xs, tile_expert, tile_valid, lw_moe['w_gu'], lw_moe['b_gu'], lw_moe['w_down'], lw_moe['b_down'])
    n_pad = -(-n // MOE_TM) * MOE_TM
    row_kmajor = jnp.pad(jnp.transpose(row.reshape(n, TOP_K)), ((0, 0), (0, n_pad - n))).reshape(TOP_K * n_pad)
    y4 = _row_gather(ys, row_kmajor)
    return y4.reshape(TOP_K, n_pad, d)


DEC_PAGES = 8


def _rows_to_headmajor(o8, width):
    wide = jnp.concatenate([o8] * H_B, axis=1)
    row = lax.broadcasted_iota(I32, wide.shape, 0)
    lane = lax.broadcasted_iota(I32, wide.shape, 1)
    return jnp.sum(jnp.where(row == lane // width, wide, 0.0), axis=0, keepdims=True)


def _head_rows(q_row, width, pad_to):
    rows = [q_row[:, hh * width:(hh + 1) * width] for hh in range(H_B)]
    q4 = jnp.concatenate(rows + [jnp.zeros((8 - H_B, width), F32)], axis=0)
    if pad_to > width:
        q4 = jnp.concatenate([q4, jnp.zeros((8, pad_to - width), F32)], axis=1)
    return q4


def _slope_rows():
    row = lax.broadcasted_iota(I32, (8, 1), 0)
    out = jnp.zeros((8, 1), F32)
    for hh in range(H_B):
        out = jnp.where(row == hh, ALIBI[hh], out)
    return out


def _nt(a, b, **kw):
    return lax.dot_general(a, b, (((1,), (1,)), ((), ())), preferred_element_type=F32, **kw)


def _decode_stream_kernel(layer, n_pages, pt_ref, qm_ref, kh_ref, cv_ref, qn_ref, gk_ref, wc_ref, gcmp_ref, wuvw_ref,
                          gout_ref, c_hbm, kr_hbm, cmp_hbm, ya_ref, oc_ref, sel_ref,
                          cbuf, krbuf, cmpbuf, kvc_scr, sem):
    b = pl.program_id(0)
    pg = DEC_PAGES
    n_chunks = n_pages // pg
    past = n_pages * LANE
    n_blk = 2 * n_pages

    def copies(chunk, slot):
        out = []
        for j in range(pg):
            page = pt_ref[b, chunk * pg + j]
            rows = pl.ds(j * LANE, LANE)
            out.append(pltpu.make_async_copy(c_hbm.at[layer, page], cbuf.at[slot, rows], sem.at[0, slot]))
            out.append(pltpu.make_async_copy(kr_hbm.at[layer, page], krbuf.at[slot, rows], sem.at[1, slot]))
            out.append(pltpu.make_async_copy(cmp_hbm.at[layer, page], cmpbuf.at[slot, rows], sem.at[2, slot]))
        return out

    for cp in copies(0, 0):
        cp.start()

    q32 = qm_ref[0].astype(F32)
    qlat = jnp.concatenate([q32[:, hh * 256:hh * 256 + D_LAT] for hh in range(H_A)] +
                           [jnp.zeros((8 - H_A, D_LAT), F32)], axis=0)
    qrot = jnp.concatenate([q32[:, hh * 256 + D_LAT + hh * D_ROPE:hh * 256 + D_LAT + (hh + 1) * D_ROPE]
                            for hh in range(H_A)] + [jnp.zeros((8 - H_A, D_ROPE), F32)], axis=0)
    kh_new = kh_ref[0].astype(F32)
    m0 = (jnp.sum(qlat * kh_new[:, :D_LAT], -1, keepdims=True) +
          jnp.sum(qrot * kh_new[:, D_LAT:D_LAT + D_ROPE], -1, keepdims=True))
    l0 = jnp.ones((8, 1), F32)
    acc0 = jnp.broadcast_to(cv_ref[0].astype(F32), (8, D_LAT))
    qlat_b, qrot_b = qlat.astype(BF16), qrot.astype(BF16)
    ones_c = jnp.ones((8, D_LAT), BF16)
    ones_r = jnp.ones((8, D_ROPE), BF16)
    gk = gk_ref[...]
    wc = wc_ref[...]

    def body(chunk, carry):
        m_prev, l_prev, acc = carry
        slot = chunk % 2

        @pl.when(chunk + 1 < n_chunks)
        def _():
            for cp in copies(chunk + 1, 1 - slot):
                cp.start()

        for cp in copies(chunk, slot):
            cp.wait()
        c = cbuf[slot]
        kr = krbuf[slot]
        ss = _nt(ones_c, (c * c).astype(BF16)) + _nt(ones_r, (kr * kr).astype(BF16))
        r = lax.rsqrt(ss * (1.0 / MLA_DIM) + EPS)
        s = (_nt(qlat_b, (c * gk).astype(BF16)) + _nt(qrot_b, kr.astype(BF16))) * r
        m_new = jnp.maximum(m_prev, jnp.max(s, -1, keepdims=True))
        p = jnp.exp(s - m_new)
        alpha = jnp.exp(m_prev - m_new)
        l_new = alpha * l_prev + jnp.sum(p, -1, keepdims=True)
        acc = alpha * acc + jnp.dot(p.astype(BF16), c.astype(BF16), preferred_element_type=F32)
        cm = cmpbuf[slot]
        ksum = jnp.sum(cm.reshape(2 * pg, BLK, LANE) * wc[None], axis=1)
        kvc_scr[pl.ds(pl.multiple_of(chunk * 2 * pg, 2 * pg), 2 * pg), :] = ksum
        return m_new, l_new, acc

    m_f, l_f, acc = lax.fori_loop(0, n_chunks, body, (m0, l0, acc0))
    o = acc / l_f
    yw = jnp.dot(o.astype(BF16), wuvw_ref[...], preferred_element_type=F32)
    row = lax.broadcasted_iota(I32, yw.shape, 0)
    lane2 = lax.broadcasted_iota(I32, yw.shape, 1)
    ya = jnp.sum(jnp.where(row == lane2 // D_VA, yw, 0.0), axis=0, keepdims=True)
    ya_ref[0] = ya * _group_rsqrt(ya, GROUP_W) * gout_ref[:, 0:256]

    lane = _lane_iota((1, LANE))
    khalf = lane < D_NSA
    ksum = kvc_scr[...]
    ssq = jnp.sum(jnp.where(khalf, ksum * ksum, 0.0), -1, keepdims=True)
    kvc = jnp.where(khalf, ksum * lax.rsqrt(ssq * (1.0 / D_NSA) + EPS) * gcmp_ref[...], ksum)
    q8 = _head_rows(qn_ref[0], D_NSA, D_NSA)
    jb = lax.broadcasted_iota(I32, (1, n_blk), 1)
    cpos = (jb + 1) * BLK - 1
    s_c = _nt(q8, kvc[:, :D_NSA], precision=HI) - _slope_rows() * (past - cpos).astype(F32)
    m_c = jnp.max(s_c, -1, keepdims=True)
    e_c = jnp.exp(s_c - m_c)
    p_c = e_c / jnp.maximum(jnp.sum(e_c, -1, keepdims=True), 1e-30)
    oc8 = jnp.dot(p_c.astype(BF16), kvc[:, D_NSA:].astype(BF16), preferred_element_type=F32)
    oc_ref[0] = _rows_to_headmajor(oc8, D_NSA)
    hrow = lax.broadcasted_iota(I32, p_c.shape, 0)
    imp = jnp.sum(jnp.where(hrow < H_B, p_c, 0.0), axis=0, keepdims=True)
    imp = jnp.where(jb == 0, FORCE_SCORE, imp)
    sel = jnp.zeros((1, LANE), I32)
    for k in range(N_SEL - 1):
        mx = jnp.max(imp, -1, keepdims=True)
        idx = jnp.min(jnp.where(imp == mx, jb, n_blk), -1, keepdims=True)
        sel = jnp.where(lane == k, idx, sel)
        imp = jnp.where(jb == idx, -3e38, imp)
    sel_ref[0] = sel


def _decode_stream(layer, page_table, qm, kh, cv, qn, lw, cache_c, cache_kr, cache_cmp):
    db, n_pages = page_table.shape
    n_blk = 2 * n_pages
    pg = DEC_PAGES

    def per(w):
        return pl.BlockSpec((1, 1, w), lambda b, pt: (b, 0, 0))

    def const(shape):
        return pl.BlockSpec(shape, lambda b, pt: (0,) * len(shape))

    anyspec = pl.BlockSpec(memory_space=pl.ANY)
    grid_spec = pltpu.PrefetchScalarGridSpec(
        num_scalar_prefetch=1,
        grid=(db,),
        in_specs=[per(1024), per(256), per(LANE), per(256), const((1, LANE)), const((BLK, LANE)), const((1, LANE)),
                  const((LANE, 256)), const((1, D_MODEL)), anyspec, anyspec, anyspec],
        out_specs=[per(256), per(256), per(LANE)],
        scratch_shapes=[pltpu.VMEM((2, pg * LANE, LANE), F32), pltpu.VMEM((2, pg * LANE, D_ROPE), F32),
                        pltpu.VMEM((2, pg * LANE, LANE), F32), pltpu.VMEM((n_blk, LANE), F32),
                        pltpu.SemaphoreType.DMA((3, 2))])
    ya, oc, sel = pl.pallas_call(
        functools.partial(_decode_stream_kernel, layer, n_pages),
        out_shape=[jax.ShapeDtypeStruct((db, 1, 256), F32), jax.ShapeDtypeStruct((db, 1, 256), F32),
                   jax.ShapeDtypeStruct((db, 1, LANE), I32)],
        grid_spec=grid_spec,
        compiler_params=_cparams(("arbitrary",)),
        name="decode_stream",
    )(page_table, qm.reshape(db, 1, 1024), kh.reshape(db, 1, 256), cv.reshape(db, 1, LANE), qn.reshape(db, 1, 256),
      lw['g_k_mla'], lw['wc'], lw['g_k_cmp'], lw['wuv_wide'], lw['g_out'], cache_c, cache_kr, cache_cmp)
    return ya.reshape(db, 256), oc, sel


def _decode_sel_kernel(layer, n_pages, w_buf, pt_ref, si_ref, qn_ref, gt_ref, oc_ref, snew_ref, wnew_ref, win_ref,
                       gout_ref, sel_hbm, yb_ref, selbuf, sem):
    b = pl.program_id(0)
    n_sel = N_SEL - 1
    past = n_pages * LANE

    def copies():
        out = []
        for r in range(n_sel):
            j = si_ref[b, r]
            page = pt_ref[b, j // 2]
            off = pl.multiple_of((j % 2) * BLK, BLK)
            out.append(pltpu.make_async_copy(sel_hbm.at[layer, page, pl.ds(off, BLK)],
                                             selbuf.at[pl.ds(r * BLK, BLK)], sem))
        return out

    for cp in copies():
        cp.start()
    q8 = _head_rows(qn_ref[0], D_NSA, LANE)
    q8b = q8.astype(BF16)
    q8r = q8b.astype(F32)
    slope = _slope_rows()
    lane = _lane_iota((1, LANE))

    def branch(kv, dist, valid, new_row):
        kvb = kv.astype(BF16)
        s = _nt(q8b, kvb) - slope * dist
        if valid is not None:
            s = jnp.where(valid, s, NEG_INF)
        newb = new_row.astype(BF16).astype(F32)
        s_new = jnp.sum(jnp.where(lane < D_NSA, q8r * newb, 0.0), -1, keepdims=True)
        m = jnp.maximum(jnp.max(s, -1, keepdims=True), s_new)
        e = jnp.exp(s - m)
        if valid is not None:
            e = jnp.where(valid, e, 0.0)
        e_new = jnp.exp(s_new - m)
        den = jnp.sum(e, -1, keepdims=True) + e_new
        o = (jnp.dot(e.astype(BF16), kvb, preferred_element_type=F32) + e_new * newb) / den
        return _rows_to_headmajor(pltpu.roll(o, D_NSA, 1)[:, :D_NSA], D_NSA)

    iw = lax.broadcasted_iota(I32, (1, w_buf), 1)
    distw = w_buf - iw
    validw = (distw < WINDOW) & (past - w_buf + iw >= 0)
    yw = branch(win_ref[0, 0], distw.astype(F32), validw, wnew_ref[0])

    for cp in copies():
        cp.wait()
    ls = lax.broadcasted_iota(I32, (1, n_sel * BLK), 1)
    spos = ls % BLK
    for r in range(n_sel):
        spos = spos + jnp.where(ls // BLK == r, si_ref[b, r] * BLK, 0)
    ys = branch(selbuf[...], (past - spos).astype(F32), None, snew_ref[0])
    g = gt_ref[0]
    y = _head_cols(g, 0) * oc_ref[0] + _head_cols(g, 1) * ys + _head_cols(g, 2) * yw
    yb_ref[0] = y * _group_rsqrt(y, GROUP_W) * gout_ref[:, 256:512]


def _decode_sel(layer, page_table, sel_idx, qn, gt, oc, sel_new, win_new, win_cache, lw, cache_sel):
    db, n_pages = page_table.shape
    w_buf = win_cache.shape[2]

    def per(w):
        return pl.BlockSpec((1, 1, w), lambda b, pt, si: (b, 0, 0))

    grid_spec = pltpu.PrefetchScalarGridSpec(
        num_scalar_prefetch=2,
        grid=(db,),
        in_specs=[per(256), per(LANE), per(256), per(LANE), per(LANE),
                  pl.BlockSpec((1, 1, w_buf, LANE), lambda b, pt, si: (layer, b, 0, 0)),
                  pl.BlockSpec((1, D_MODEL), lambda b, pt, si: (0, 0)),
                  pl.BlockSpec(memory_space=pl.ANY)],
        out_specs=per(256),
        scratch_shapes=[pltpu.VMEM(((N_SEL - 1) * BLK, LANE), F32), pltpu.SemaphoreType.DMA])
    yb = pl.pallas_call(
        functools.partial(_decode_sel_kernel, layer, n_pages, w_buf),
        out_shape=jax.ShapeDtypeStruct((db, 1, 256), F32),
        grid_spec=grid_spec,
        compiler_params=_cparams(("arbitrary",)),
        name="decode_sel",
    )(page_table, sel_idx, qn.reshape(db, 1, 256), gt.reshape(db, 1, LANE), oc, sel_new.reshape(db, 1, LANE),
      win_new.reshape(db, 1, LANE), win_cache, lw['g_out'], cache_sel)
    return yb.reshape(db, 256)


def kernel(x_prompt, x_sample, cache_mla_c, cache_mla_kr, cache_nsa_cmp_kv, cache_nsa_sel_kv, cache_nsa_win_kv,
           state_pool, page_table, c_prompt, c_sample, w_ada, b_ada, norm1, norm2, w_in, w_uk, w_uv, g_kv, g_q_mla,
           g_k_mla, g_q_nsa, g_k_cmp, g_k_sel, g_k_win, w_cmp_k, w_cmp_v, w_pool, pool_scale, g_sgu, w_s, b_s,
           g_out, w_out, w_router, b_router, w_gu, b_gu, w_down, b_down):
    w = dict(norm1=norm1, norm2=norm2, w_in=w_in, w_uk=w_uk, w_uv=w_uv, g_kv=g_kv, g_q_mla=g_q_mla, g_k_mla=g_k_mla,
             g_q_nsa=g_q_nsa, g_k_cmp=g_k_cmp, g_k_sel=g_k_sel, g_k_win=g_k_win, w_cmp_k=w_cmp_k, w_cmp_v=w_cmp_v,
             w_pool=w_pool, pool_scale=pool_scale, g_sgu=g_sgu, w_s=w_s, b_s=b_s, g_out=g_out, w_out=w_out,
             w_router=w_router, b_router=b_router, w_gu=w_gu, b_gu=b_gu, w_down=w_down, b_down=b_down)
    nb, t, d = x_prompt.shape
    db = x_sample.shape[0]
    depth = w_ada.shape[0]
    n_tok = nb * t
    past = page_table.shape[1] * cache_mla_c.shape[2]
    w_keep = min(WINDOW, t)

    rows = -(-(nb + db) // 8) * 8
    c_all = jnp.concatenate([c_prompt, c_sample, jnp.zeros((rows - nb - db, d), F32)], axis=0)
    ada = _ada(c_all, w_ada, b_ada)

    xp = x_prompt
    xs = x_sample.reshape(1, db, d)
    st = [[] for _ in range(13)]
    for l in range(depth):
        lw = _layer_weights(l, w)
        sh1p, sc1p, g1p, sh2p, sc2p, g2p = [z[:, None, :] for z in jnp.split(ada[l, :nb], 6, axis=-1)]
        sh1s, sc1s, g1s, sh2s, sc2s, g2s = jnp.split(ada[l, nb:nb + db], 6, axis=-1)

        o = _inproj(xp, sh1p, sc1p, lw, 0)
        ya = _mla_prompt(o['qm'], o['kh'], o['cv'], lw)
        yb = _nsa_prompt(o['qn'], o['gt'], o['kvc'], o['selb'], o['winb'], lw)
        ys = [y.reshape(n_tok, GROUP_W) for y in (ya, yb, o['yc'], o['yd'])]
        x1p, h2p, tip, twp = _outproj(ys, xp.reshape(n_tok, d), g1p, sc2p, sh2p, lw, t)

        s = _inproj(xs, sh1s[None], sc1s[None], lw, past, pool_buf=state_pool[l])
        ya_s, oc_s, sel_s = _decode_stream(l, page_table, s['qm'][0], s['kh'][0], s['cv'][0], s['qn'][0], lw,
                                           cache_mla_c, cache_mla_kr, cache_nsa_cmp_kv)
        yb_s = _decode_sel(l, page_table, sel_s.reshape(db, LANE), s['qn'][0], s['gt'][0], oc_s, s['sel'][0],
                           s['win'][0], cache_nsa_win_kv, lw, cache_nsa_sel_kv)
        x1s, h2s, tis, tws = _outproj([ya_s, yb_s, s['yc'][0], s['yd'][0]], xs[0], g1s, sc2s, sh2s, lw, 1)

        y4 = _moe(jnp.concatenate([h2p, h2s], axis=0), jnp.concatenate([tip, tis], axis=0), lw['moe'])
        tw_all = jnp.concatenate([twp, tws], axis=0)
        xp = _combine(x1p, y4, tw_all, g2p, 0, t).reshape(nb, t, d)
        xs = _combine(x1s, y4, tw_all, g2s, n_tok, 1).reshape(1, db, d)

        per_l = (o['cl'], s['cl'][0][:, None, :], o['kr'], s['kr'][0][:, None, :], o['cmp'], s['cmp'][0][:, None, :],
                 o['sel'], s['sel'][0][:, None, :], o['win'][:, t - w_keep:],
                 jnp.concatenate([cache_nsa_win_kv[l][:, 1:], s['win'][0][:, None, :]], axis=1),
                 o['zc_tail'][:, 16 - POOL_PAD:],
                 jnp.concatenate([state_pool[l][:, 1:], s['zc_tail'][0][:, None, :]], axis=1),
                 s['v_tail'][0][:, None, :])
        for k, v in enumerate(per_l):
            st[k].append(v)
    return (xp, xs.reshape(db, 1, d)) + tuple(jnp.stack(v, axis=0) for v in st)
```

```python
import functools

import numpy as np
import jax
import jax.numpy as jnp
from jax import lax
from jax.experimental import pallas as pl
from jax.experimental.pallas import tpu as pltpu

F32 = jnp.float32
BF16 = jnp.bfloat16
I32 = jnp.int32
HI = lax.Precision.HIGHEST

EPS = 1e-6
NEG_INF = -1e30
D_MODEL = 1024
GROUP_W = 256
H_A, D_NOPE, D_ROPE, D_LAT, D_VA = 4, 64, 32, 128, 64
MLA_DIM = D_LAT + D_ROPE
MLA_SCALE = (D_NOPE + D_ROPE) ** -0.5
ROPE_BASE = 10000.0
H_B, D_NSA, BLK, N_SEL, WINDOW = 4, 64, 64, 16, 512
NSA_SCALE = D_NSA ** -0.5
FORCE_SCORE = 1e4
POOL_WINDOWS = (2, 4, 8, 16)
POOL_PAD = 15
H_D, CHUNK, D_VD = 4, 128, 64
N_EXP, TOP_K, D_FF = 32, 4, 1024
SWIGLU_LIMIT, SWIGLU_ALPHA = 7.0, 1.702
ALIBI = tuple(2.0 ** (-8.0 * (i + 1) / H_B) for i in range(H_B))

LANE = 128
VMEM_LIMIT = 56 * 1024 * 1024

C_QN, C_QR, C_QRS, C_CA, C_KR, C_KRS, C_QB, C_CMP, C_SEL, C_WIN, C_ZG, C_ZC, C_ZD, N_PAD = (
    0, 256, 384, 512, 640, 768, 896, 1152, 1280, 1408, 1536, 1664, 1920, 2432)


def _in_cols():
    qa, ca, ra, qb = 0, 384, 512, 544
    cmp_, sel, win, zg, zc, zd, n_in = 800, 928, 1056, 1184, 1196, 1452, 1964
    per = D_NOPE + D_ROPE
    half = D_ROPE // 2
    cols = []
    for h in range(H_A):
        cols += list(range(qa + h * per, qa + h * per + D_NOPE))
    for h in range(H_A):
        cols += list(range(qa + h * per + D_NOPE, qa + (h + 1) * per))
    for h in range(H_A):
        b = qa + h * per + D_NOPE
        cols += list(range(b + half, b + D_ROPE)) + list(range(b, b + half))
    cols += list(range(ca, ca + D_LAT))
    cols += list(range(ra, ra + D_ROPE)) * 4
    cols += (list(range(ra + half, ra + D_ROPE)) + list(range(ra, ra + half))) * 4
    cols += list(range(qb, qb + 256))
    cols += list(range(cmp_, cmp_ + 128)) + list(range(sel, sel + 128)) + list(range(win, win + 128))
    cols += list(range(zg, zg + 12)) + [n_in] * (LANE - 12)
    cols += list(range(zc, zc + 256))
    cols += list(range(zd, zd + 512))
    assert len(cols) == N_PAD
    return np.asarray(cols, np.int32)


_IN_COLS = _in_cols()


def _cparams(sem):
    return pltpu.CompilerParams(dimension_semantics=sem, vmem_limit_bytes=VMEM_LIMIT)


def _lane_iota(shape):
    return lax.broadcasted_iota(I32, shape, len(shape) - 1)


def _group_rsqrt(x, width):
    n = x.shape[-1]
    sq = x * x
    if width == n:
        return lax.rsqrt(jnp.sum(sq, -1, keepdims=True) * (1.0 / width) + EPS)
    lane = _lane_iota((1, n))
    out = jnp.zeros_like(x)
    for g in range(n // width):
        m = (lane >= g * width) & (lane < (g + 1) * width)
        ss = jnp.sum(jnp.where(m, sq, 0.0), -1, keepdims=True)
        out = jnp.where(m, lax.rsqrt(ss * (1.0 / width) + EPS), out)
    return out


def _ada_kernel(c_ref, w_ref, b_ref, o_ref):
    c = c_ref[...]
    s = c * jax.nn.sigmoid(c)
    o_ref[0] = jnp.dot(s.astype(BF16), w_ref[0].astype(BF16), preferred_element_type=F32) + b_ref[0]


def _ada(c_all, w_ada, b_ada):
    depth, d, n = w_ada.shape
    rows = c_all.shape[0]
    tn = 1536
    return pl.pallas_call(
        _ada_kernel,
        out_shape=jax.ShapeDtypeStruct((depth, rows, n), F32),
        grid=(depth, n // tn),
        in_specs=[pl.BlockSpec((rows, d), lambda l, j: (0, 0)),
                  pl.BlockSpec((1, d, tn), lambda l, j: (l, 0, j)),
                  pl.BlockSpec((1, 1, tn), lambda l, j: (l, 0, j))],
        out_specs=pl.BlockSpec((1, rows, tn), lambda l, j: (l, 0, j)),
        compiler_params=_cparams(("arbitrary", "arbitrary")),
        name="ada",
    )(c_all, w_ada, b_ada.reshape(depth, 1, n))


def _gelu_tanh(x):
    return 0.5 * x * (1.0 + jnp.tanh(0.7978845608028654 * (x + 0.044715 * (x * x * x))))


def _inproj_kernel(sample, tm, x_ref, sh_ref, sc_ref, n1_ref, w_ref, cos_ref, sin_ref, wuk_ref, gq_ref,
                   gkv_ref, gk_ref, gqn_ref, gcmp_ref, gsel_ref, gwin_ref, wc_ref, wpool_ref, pscale_ref,
                   gsgu_ref, ws_ref, bs_ref, gout_ref, pool_ref,
                   qm_ref, kh_ref, cl_ref, cv_ref, kr_ref, qn_ref, gt_ref, cmp_ref, sel_ref, selb_ref,
                   win_ref, winb_ref, kvc_ref, zc_ref, v_ref, yc_ref, yd_ref, carry_ref):
    i = pl.program_id(1)
    x = x_ref[0]
    xn = x * lax.rsqrt(jnp.mean(x * x, -1, keepdims=True) + EPS) * n1_ref[...]
    h = xn * (1.0 + sc_ref[0]) + sh_ref[0]
    z = jnp.dot(h.astype(BF16), w_ref[...], preferred_element_type=F32)
    cos4, sin4 = cos_ref[...], sin_ref[...]
    lane = _lane_iota((1, LANE))

    qlat = jnp.dot(z[:, C_QN:C_QN + 256].astype(BF16), wuk_ref[...], preferred_element_type=F32)
    qrot = z[:, C_QR:C_QR + LANE] * cos4 + z[:, C_QRS:C_QRS + LANE] * sin4
    for hh in range(H_A):
        lat = qlat[:, hh * D_LAT:(hh + 1) * D_LAT]
        rot = jnp.where((lane >= hh * D_ROPE) & (lane < (hh + 1) * D_ROPE), qrot, 0.0)
        ss = jnp.sum(lat * lat, -1, keepdims=True) + jnp.sum(rot * rot, -1, keepdims=True)
        r = lax.rsqrt(ss * (1.0 / MLA_DIM) + EPS) * MLA_SCALE
        qm_ref[0, :, hh * 256:hh * 256 + D_LAT] = (lat * r * gq_ref[:, hh * D_LAT:(hh + 1) * D_LAT]).astype(BF16)
        qm_ref[0, :, hh * 256 + D_LAT:(hh + 1) * 256] = (rot * r).astype(BF16)

    ca = z[:, C_CA:C_CA + LANE]
    c_lat = ca * _group_rsqrt(ca, LANE) * gkv_ref[...]
    kr4 = z[:, C_KR:C_KR + LANE] * cos4 + z[:, C_KRS:C_KRS + LANE] * sin4
    cl_ref[0] = c_lat
    ones_blk = jnp.ones((tm, LANE), BF16)
    cv_ref[0, :, :LANE] = c_lat.astype(BF16)
    cv_ref[0, :, LANE:] = ones_blk
    kr_ref[0] = kr4[:, :D_ROPE]
    ssk = jnp.sum(c_lat * c_lat, -1, keepdims=True) + 0.25 * jnp.sum(kr4 * kr4, -1, keepdims=True)
    rk = lax.rsqrt(ssk * (1.0 / MLA_DIM) + EPS)
    kh_ref[0, :, :LANE] = (c_lat * rk * gk_ref[...]).astype(BF16)
    kh_ref[0, :, LANE:] = (kr4 * rk).astype(BF16)

    zqb = z[:, C_QB:C_QB + 256]
    qn_ref[0] = zqb * _group_rsqrt(zqb, D_NSA) * gqn_ref[...] * NSA_SCALE
    gt_ref[0] = jax.nn.sigmoid(z[:, C_ZG:C_ZG + LANE])
    cmp_rows = z[:, C_CMP:C_CMP + LANE]
    cmp_ref[0] = cmp_rows
    khalf = lane < D_NSA

    def norm_k(zz, g_ref):
        ss = jnp.sum(jnp.where(khalf, zz * zz, 0.0), -1, keepdims=True)
        return jnp.where(khalf, zz * lax.rsqrt(ss * (1.0 / D_NSA) + EPS) * g_ref[...], zz)

    sel_rows = norm_k(z[:, C_SEL:C_SEL + LANE], gsel_ref)
    win_rows = norm_k(z[:, C_WIN:C_WIN + LANE], gwin_ref)
    sel_ref[0] = sel_rows
    selb_ref[0, :, :LANE] = sel_rows.astype(BF16)
    selb_ref[0, :, LANE:] = ones_blk
    win_ref[0] = win_rows
    winb_ref[0, :, :LANE] = win_rows.astype(BF16)
    winb_ref[0, :, LANE:] = ones_blk

    zc = z[:, C_ZC:C_ZC + 256]
    zd = z[:, C_ZD:C_ZD + 512]
    uv = _gelu_tanh(zd)
    u = uv[:, :256]
    v = uv[:, 256:]
    v = v * _group_rsqrt(v, D_VD) * gsgu_ref[...]
    lane2 = _lane_iota((1, 256))
    wsel = [lane2 < 64, lane2 < 128, lane2 < 192]

    def pick(a2, a4, a8, a16):
        return jnp.where(wsel[0], a2, jnp.where(wsel[1], a4, jnp.where(wsel[2], a8, a16)))

    wlane = pick(2.0, 4.0, 8.0, 16.0)
    if not sample:
        nbk = tm // BLK
        ksum = jnp.sum(cmp_rows.reshape(nbk, BLK, LANE) * wc_ref[...][None], axis=1)
        kvc_ref[0] = norm_k(ksum, gcmp_ref)

        @pl.when(i == 0)
        def _():
            carry_ref[...] = jnp.zeros_like(carry_ref)

        ext = jnp.concatenate([carry_ref[...], zc], axis=0)
        a2 = ext + pltpu.roll(ext, 1, 0)
        a4 = a2 + pltpu.roll(a2, 2, 0)
        a8 = a4 + pltpu.roll(a4, 4, 0)
        a16 = a8 + pltpu.roll(a8, 8, 0)
        wsum = pick(a2, a4, a8, a16)[16:]
        carry_ref[...] = zc[tm - 16:]
        pos = (i * tm + lax.broadcasted_iota(I32, (tm, 1), 0)).astype(F32)
        cnt = jnp.minimum(pos + 1.0, wlane)
        zc_ref[0] = zc[tm - 16:]
        v_ref[0] = v[tm - 16:]
        for c in range(tm // CHUNK):
            vc = v[c * CHUNK:(c + 1) * CHUNK]
            vst = jnp.concatenate(
                [jnp.where((lane2 >= hh * D_VD) & (lane2 < (hh + 1) * D_VD), vc, 0.0) for hh in range(H_D)], axis=0)
            mixed = jnp.dot(ws_ref[...], vst.astype(BF16), preferred_element_type=F32) + bs_ref[...]
            yd = u[c * CHUNK:(c + 1) * CHUNK] * mixed
            yd_ref[0, c * CHUNK:(c + 1) * CHUNK, :] = (
                yd * _group_rsqrt(yd, GROUP_W) * gout_ref[:, 768:1024]).astype(BF16)
    else:
        kvc_ref[0] = jnp.zeros_like(kvc_ref[0])
        pb = pool_ref[...]
        sums = []
        for w in POOL_WINDOWS:
            sums.append(zc + jnp.sum(pb[:, POOL_PAD - (w - 1):, :], axis=1))
        wsum = pick(*sums)
        cnt = wlane
        zc_ref[0] = zc
        v_ref[0] = v
        mixed = ws_ref[...] * v + bs_ref[...]
        yd = u * mixed
        yd_ref[0] = (yd * _group_rsqrt(yd, GROUP_W) * gout_ref[:, 768:1024]).astype(BF16)
    d = wsum / cnt - zc
    yc = jnp.dot(d.astype(BF16), wpool_ref[...], preferred_element_type=F32) * pscale_ref[...]
    yc_ref[0] = (yc * _group_rsqrt(yc, GROUP_W) * gout_ref[:, 512:768]).astype(BF16)


def _inproj(x, sh1, sc1, lw, pos0, pool_buf=None):
    sample = pool_buf is not None
    nb, t, d = x.shape
    tm = t if sample else min(512, t)
    nt = t // tm
    half = D_ROPE // 2
    freqs = ROPE_BASE ** (-jnp.arange(half, dtype=F32) / half)
    pos = (jnp.full((t,), pos0, I32) if sample else pos0 + jnp.arange(t, dtype=I32)).astype(F32)
    ang = pos[:, None] * freqs[None, :]
    cos, sin = jnp.cos(ang), jnp.sin(ang)
    cos4 = jnp.tile(jnp.concatenate([cos, cos], -1), (1, 4))
    sin4 = jnp.tile(jnp.concatenate([-sin, sin], -1), (1, 4))
    if sample:
        ws = jnp.repeat(lw['w_s'][:, 0, 0], D_VD)[None, :]
        bs = jnp.repeat(lw['b_s'][:, 0], D_VD)[None, :]
        pool = pool_buf
        pool_spec = pl.BlockSpec(pool.shape, lambda b, i: (0, 0, 0))
        ws_spec = pl.BlockSpec((1, 256), lambda b, i: (0, 0))
        bs_spec = pl.BlockSpec((1, 256), lambda b, i: (0, 0))
        tail = tm
    else:
        ws, bs = lw['ws_cat'], lw['bs_full']
        pool = jnp.zeros((1, 8, LANE), F32)
        pool_spec = pl.BlockSpec(pool.shape, lambda b, i: (0, 0, 0))
        ws_spec = pl.BlockSpec((CHUNK, 4 * CHUNK), lambda b, i: (0, 0))
        bs_spec = pl.BlockSpec((CHUNK, 256), lambda b, i: (0, 0))
        tail = 16
    nbk = max(tm // BLK, 8)

    def row(w):
        return pl.BlockSpec((1, tm, w), lambda b, i: (b, i, 0))

    def const(shape):
        return pl.BlockSpec(shape, lambda b, i: (0,) * len(shape))

    def tailspec(w):
        return pl.BlockSpec((1, tail, w), lambda b, i: (b, 0, 0))

    mod = row(d) if sample else pl.BlockSpec((1, 1, d), lambda b, i: (b, 0, 0))
    in_specs = [row(d), mod, mod,
                const((1, d)), const((d, N_PAD)),
                pl.BlockSpec((tm, LANE), lambda b, i: (i, 0)), pl.BlockSpec((tm, LANE), lambda b, i: (i, 0)),
                const((256, 512)), const((1, 512)), const((1, LANE)), const((1, LANE)), const((1, 256)),
                const((1, LANE)), const((1, LANE)), const((1, LANE)), const((BLK, LANE)), const((256, 256)),
                const((1, 256)), const((1, 256)), ws_spec, bs_spec, const((1, d)), pool_spec]
    outs = [('qm', 1024, BF16), ('kh', 256, BF16), ('cl', LANE, F32), ('cv', 256, BF16), ('kr', D_ROPE, F32),
            ('qn', 256, F32), ('gt', LANE, F32), ('cmp', LANE, F32), ('sel', LANE, F32), ('selb', 256, BF16),
            ('win', LANE, F32), ('winb', 256, BF16)]
    out_shape = [jax.ShapeDtypeStruct((nb, t, w), dt) for _, w, dt in outs]
    out_specs = [row(w) for _, w, _ in outs]
    out_shape += [jax.ShapeDtypeStruct((nb, nt * nbk, LANE), F32)]
    out_specs += [pl.BlockSpec((1, nbk, LANE), lambda b, i: (b, i, 0))]
    out_shape += [jax.ShapeDtypeStruct((nb, tail, 256), F32), jax.ShapeDtypeStruct((nb, tail, 256), F32)]
    out_specs += [tailspec(256), tailspec(256)]
    out_shape += [jax.ShapeDtypeStruct((nb, t, 256), BF16), jax.ShapeDtypeStruct((nb, t, 256), BF16)]
    out_specs += [row(256), row(256)]
    res = pl.pallas_call(
        functools.partial(_inproj_kernel, sample, tm),
        out_shape=out_shape,
        grid=(nb, nt),
        in_specs=in_specs,
        out_specs=out_specs,
        scratch_shapes=[pltpu.VMEM((16, 256), F32)],
        compiler_params=_cparams(("arbitrary", "arbitrary")),
        name="inproj_sample" if sample else "inproj_prompt",
    )(x, sh1, sc1, lw['norm1'], lw['w_pad'], cos4, sin4, lw['wuk_bd'], lw['gq_mla'], lw['g_kv'], lw['g_k_mla'],
      lw['gq_nsa'], lw['g_k_cmp'], lw['g_k_sel'], lw['g_k_win'], lw['wc'], lw['wpool_bd'], lw['pool_scale'],
      lw['g_sgu'], ws, bs, lw['g_out'], pool)
    names = [n for n, _, _ in outs] + ['kvc', 'zc_tail', 'v_tail', 'yc', 'yd']
    return dict(zip(names, res))


def _block_diag(blocks):
    n = len(blocks)
    r, c = blocks[0].shape
    out = jnp.zeros((n * r, n * c), blocks[0].dtype)
    for k, b in enumerate(blocks):
        out = out.at[k * r:(k + 1) * r, k * c:(k + 1) * c].set(b)
    return out


def _pad_lanes(v, n=LANE):
    return jnp.concatenate([v, jnp.ones((n - v.shape[0],), v.dtype)])[None, :]


def _layer_weights(l, w):
    lw = {}
    w_in = jnp.concatenate([w['w_in'][l], jnp.zeros((D_MODEL, 1), F32)], axis=1)
    lw['w_pad'] = jnp.take(w_in, _IN_COLS, axis=1).astype(BF16)
    lw['norm1'] = w['norm1'][l][None, :]
    lw['norm2'] = w['norm2'][l][None, :]
    lw['wuk_bd'] = _block_diag([w['w_uk'][l][h] for h in range(H_A)]).astype(BF16)
    lw['wuv_bd'] = _block_diag([w['w_uv'][l][h] for h in range(H_A)]).astype(BF16)
    lw['wuv_wide'] = jnp.concatenate([w['w_uv'][l][h] for h in range(H_A)], axis=1).astype(BF16)
    lw['gq_mla'] = w['g_q_mla'][l].reshape(1, H_A * D_LAT)
    lw['g_kv'] = w['g_kv'][l][None, :]
    lw['g_k_mla'] = w['g_k_mla'][l][None, :]
    lw['gq_nsa'] = jnp.tile(w['g_q_nsa'][l], H_B)[None, :]
    lw['g_k_cmp'] = _pad_lanes(w['g_k_cmp'][l])
    lw['g_k_sel'] = _pad_lanes(w['g_k_sel'][l])
    lw['g_k_win'] = _pad_lanes(w['g_k_win'][l])
    lw['wc'] = jnp.concatenate([jnp.tile(w['w_cmp_k'][l][:, None], (1, D_NSA)),
                                jnp.tile(w['w_cmp_v'][l][:, None], (1, D_NSA))], axis=1)
    lw['wpool_bd'] = _block_diag([w['w_pool'][l][g] for g in range(4)]).astype(BF16)
    lw['pool_scale'] = w['pool_scale'][l][None, :]
    lw['g_sgu'] = w['g_sgu'][l][None, :]
    tril = jnp.tril(jnp.ones((CHUNK, CHUNK), F32))
    lw['ws_cat'] = jnp.concatenate([w['w_s'][l][h] * tril for h in range(H_D)], axis=1).astype(BF16)
    lw['bs_full'] = jnp.repeat(jnp.transpose(w['b_s'][l]), D_VD, axis=1)
    lw['w_s'] = w['w_s'][l]
    lw['b_s'] = w['b_s'][l]
    lw['g_out'] = w['g_out'][l][None, :]
    lw['w_out'] = w['w_out'][l].astype(BF16)
    lw['w_router'] = jnp.pad(w['w_router'][l], ((0, 0), (0, LANE - N_EXP)))
    lw['b_router'] = jnp.concatenate([w['b_router'][l], jnp.full((LANE - N_EXP,), NEG_INF, F32)])[None, :]
    lw['moe'] = {k: w[k][l] for k in ('w_gu', 'b_gu', 'w_down', 'b_down')}
    return lw


def _causal_pairs(nq):
    qi = np.concatenate([np.full((q + 1,), q, np.int32) for q in range(nq)])
    ki = np.concatenate([np.arange(q + 1, dtype=np.int32) for q in range(nq)])
    return jnp.asarray(qi), jnp.asarray(ki)


def _flash_step(s, v_ext, m_ref, acc_ref):
    m_prev = m_ref[...]
    m_new = jnp.maximum(m_prev, jnp.max(s, -1, keepdims=True))
    m_use = jnp.maximum(m_new, 0.1 * NEG_INF)
    p = jnp.exp(s - jnp.concatenate([m_use] * (s.shape[1] // LANE), axis=1))
    alpha = jnp.exp(m_prev - m_new)
    acc_ref[...] = (jnp.concatenate([alpha, alpha], axis=1) * acc_ref[...] +
                    jnp.dot(p.astype(BF16), v_ext, preferred_element_type=F32))
    m_ref[...] = m_new


def _mla_kernel(tq, qi_ref, ki_ref, q_ref, k_ref, v_ref, wuv_ref, gout_ref, o_ref, q4_ref, m_ref, acc_ref):
    p_id = pl.program_id(1)
    qi, ki = qi_ref[p_id], ki_ref[p_id]

    @pl.when(ki == 0)
    def _():
        for hh in range(H_A):
            q4_ref[hh * tq:(hh + 1) * tq, :] = q_ref[0, :, hh * 256:(hh + 1) * 256]
        m_ref[...] = jnp.full_like(m_ref, NEG_INF)
        acc_ref[...] = jnp.zeros_like(acc_ref)

    def update(masked):
        s = lax.dot_general(q4_ref[...], k_ref[0], (((1,), (1,)), ((), ())), preferred_element_type=F32)
        if masked:
            row = lax.broadcasted_iota(I32, (tq, tq), 0)
            col = lax.broadcasted_iota(I32, (tq, tq), 1)
            keep = jnp.concatenate([col <= row] * H_A, axis=0)
            s = jnp.where(keep, s, NEG_INF)
        _flash_step(s, v_ref[0], m_ref, acc_ref)

    @pl.when(ki < qi)
    def _():
        update(False)

    @pl.when(ki == qi)
    def _():
        update(True)
        acc = acc_ref[...]
        o = acc[:, :LANE] / acc[:, LANE:]
        ocat = jnp.concatenate([o[hh * tq:(hh + 1) * tq] for hh in range(H_A)], axis=1)
        ya = jnp.dot(ocat.astype(BF16), wuv_ref[...], preferred_element_type=F32)
        o_ref[0] = (ya * _group_rsqrt(ya, GROUP_W) * gout_ref[:, 0:256]).astype(BF16)


def _mla_prompt(qm, kh, cv, lw):
    nb, t, _ = qm.shape
    tq = min(512, t)
    nq = t // tq
    qi, ki = _causal_pairs(nq)
    grid_spec = pltpu.PrefetchScalarGridSpec(
        num_scalar_prefetch=2,
        grid=(nb, qi.shape[0]),
        in_specs=[pl.BlockSpec((1, tq, 1024), lambda b, p, qi, ki: (b, qi[p], 0)),
                  pl.BlockSpec((1, tq, 256), lambda b, p, qi, ki: (b, ki[p], 0)),
                  pl.BlockSpec((1, tq, 256), lambda b, p, qi, ki: (b, ki[p], 0)),
                  pl.BlockSpec((512, 256), lambda b, p, qi, ki: (0, 0)),
                  pl.BlockSpec((1, D_MODEL), lambda b, p, qi, ki: (0, 0))],
        out_specs=pl.BlockSpec((1, tq, 256), lambda b, p, qi, ki: (b, qi[p], 0)),
        scratch_shapes=[pltpu.VMEM((H_A * tq, 256), BF16), pltpu.VMEM((H_A * tq, LANE), F32),
                        pltpu.VMEM((H_A * tq, 256), F32)])
    return pl.pallas_call(
        functools.partial(_mla_kernel, tq),
        out_shape=jax.ShapeDtypeStruct((nb, t, 256), BF16),
        grid_spec=grid_spec,
        compiler_params=_cparams(("arbitrary", "arbitrary")),
        name="mla_prompt",
    )(qi, ki, qm, kh, cv, lw['wuv_bd'], lw['g_out'])


def _topk_mask(imp, jb, n_blocks, k):
    sel = jnp.zeros(imp.shape, F32)
    for _ in range(k):
        m = jnp.max(imp, -1, keepdims=True)
        idx = jnp.min(jnp.where(imp == m, jb, n_blocks), -1, keepdims=True)
        pick = jb == idx
        sel = jnp.where(pick, 1.0, sel)
        imp = jnp.where(pick, -3e38, imp)
    return sel


def _masked_softmax(s, keep):
    s = jnp.where(keep, s, NEG_INF)
    m = jnp.max(s, -1, keepdims=True)
    e = jnp.where(keep, jnp.exp(s - m), 0.0)
    return e / jnp.maximum(jnp.sum(e, -1, keepdims=True), 1e-30)


def _merge_heads(parts):
    lane = _lane_iota((1, LANE))
    lo = lane < D_NSA
    pair = [jnp.where(lo, pltpu.roll(parts[2 * j], D_NSA, 1), parts[2 * j + 1]) for j in range(2)]
    return jnp.concatenate(pair, axis=1)


def _head_cols(g, offset):
    lane2 = _lane_iota((1, 256))
    out = jnp.zeros((g.shape[0], 256), F32)
    for hh in range(H_B):
        col = g[:, 3 * hh + offset:3 * hh + offset + 1]
        out = jnp.where((lane2 >= hh * D_NSA) & (lane2 < (hh + 1) * D_NSA), col, out)
    return out


def _nsa_kernel(tq, n_blocks, wt, qi_ref, ki_ref, qn_ref, gt_ref, kvc_ref, selb_ref, winb_ref, gout_ref, o_ref,
                q4_ref, selm_ref, oc_ref, ms_ref, as_ref, mw_ref, aw_ref):
    p_id = pl.program_id(1)
    qi, ki = qi_ref[p_id], ki_ref[p_id]
    qpos = qi * tq + lax.broadcasted_iota(I32, (tq, 1), 0)

    @pl.when(ki == 0)
    def _():
        q = qn_ref[0]
        kvc = kvc_ref[0]
        kc = kvc[:, :D_NSA]
        vc = kvc[:, D_NSA:].astype(BF16)
        jb = lax.broadcasted_iota(I32, (1, n_blocks), 1)
        cpos = (jb + 1) * BLK - 1
        valid = cpos <= qpos
        dist = (qpos - cpos).astype(F32)
        imp = jnp.zeros((tq, n_blocks), F32)
        for hh in range(H_B):
            qh = q[:, hh * D_NSA:(hh + 1) * D_NSA]
            s = lax.dot_general(qh, kc, (((1,), (1,)), ((), ())), precision=HI, preferred_element_type=F32)
            pc = _masked_softmax(s - ALIBI[hh] * dist, valid)
            oc_ref[:, hh * D_NSA:(hh + 1) * D_NSA] = jnp.dot(pc.astype(BF16), vc, preferred_element_type=F32)
            imp = imp + pc
            q4_ref[hh * tq:(hh + 1) * tq, :] = jnp.concatenate(
                [qh, jnp.zeros((tq, LANE - D_NSA), F32)], axis=1).astype(BF16)
        cur = qpos // BLK
        imp = jnp.where((jb == 0) | (jb == cur), FORCE_SCORE, jnp.where(jb > cur, -1.0, imp))
        selm_ref[...] = _topk_mask(imp, jb, n_blocks, min(N_SEL, n_blocks)).astype(BF16)
        for m_r, a_r in ((ms_ref, as_ref), (mw_ref, aw_ref)):
            m_r[...] = jnp.full_like(m_r, NEG_INF)
            a_r[...] = jnp.zeros_like(a_r)

    kpos = ki * tq + lax.broadcasted_iota(I32, (1, tq), 1)
    disti = qpos - kpos
    kposf = kpos.astype(F32)

    def branch(kv, allowed, m_r, a_r):
        pen = jnp.where(allowed, 0.0, NEG_INF)
        s = lax.dot_general(q4_ref[...], kv[:, :LANE], (((1,), (1,)), ((), ())), preferred_element_type=F32)
        s = jnp.concatenate([s[hh * tq:(hh + 1) * tq] + (pen + ALIBI[hh] * kposf) for hh in range(H_B)], axis=0)
        _flash_step(s, kv, m_r, a_r)

    expand = (lax.broadcasted_iota(I32, (n_blocks, tq), 0) ==
              (ki * tq + lax.broadcasted_iota(I32, (n_blocks, tq), 1)) // BLK).astype(BF16)
    chosen = jnp.dot(selm_ref[...], expand, preferred_element_type=F32) > 0.5
    branch(selb_ref[0], chosen & (disti >= 0), ms_ref, as_ref)

    @pl.when(ki >= qi - wt)
    def _():
        branch(winb_ref[0], (disti >= 0) & (disti < WINDOW), mw_ref, aw_ref)

    @pl.when(ki == qi)
    def _():
        a_s, a_w = as_ref[...], aw_ref[...]
        o_s = a_s[:, :LANE] / jnp.maximum(a_s[:, LANE:], 1e-30)
        o_w = a_w[:, :LANE] / jnp.maximum(a_w[:, LANE:], 1e-30)
        g = gt_ref[0]
        ys = _merge_heads([o_s[hh * tq:(hh + 1) * tq] for hh in range(H_B)])
        yw = _merge_heads([o_w[hh * tq:(hh + 1) * tq] for hh in range(H_B)])
        y = _head_cols(g, 0) * oc_ref[...] + _head_cols(g, 1) * ys + _head_cols(g, 2) * yw
        o_ref[0] = (y * _group_rsqrt(y, GROUP_W) * gout_ref[:, 256:512]).astype(BF16)


def _nsa_prompt(qn, gt, kvc, selb, winb, lw):
    nb, t, _ = qn.shape
    tq = min(512, t)
    nq = t // tq
    n_blocks = t // BLK
    wt = WINDOW // tq
    qi, ki = _causal_pairs(nq)

    def qmap(b, p, qi, ki):
        return (b, qi[p], 0)

    def kmap(b, p, qi, ki):
        return (b, ki[p], 0)

    def wmap(b, p, qi, ki):
        return (b, jnp.maximum(ki[p], qi[p] - wt), 0)

    grid_spec = pltpu.PrefetchScalarGridSpec(
        num_scalar_prefetch=2,
        grid=(nb, qi.shape[0]),
        in_specs=[pl.BlockSpec((1, tq, 256), qmap), pl.BlockSpec((1, tq, LANE), qmap),
                  pl.BlockSpec((1, n_blocks, LANE), lambda b, p, qi, ki: (b, 0, 0)),
                  pl.BlockSpec((1, tq, 256), kmap), pl.BlockSpec((1, tq, 256), wmap),
                  pl.BlockSpec((1, D_MODEL), lambda b, p, qi, ki: (0, 0))],
        out_specs=pl.BlockSpec((1, tq, 256), qmap),
        scratch_shapes=[pltpu.VMEM((H_B * tq, LANE), BF16), pltpu.VMEM((tq, n_blocks), BF16),
                        pltpu.VMEM((tq, 256), F32),
                        pltpu.VMEM((H_B * tq, LANE), F32), pltpu.VMEM((H_B * tq, 256), F32),
                        pltpu.VMEM((H_B * tq, LANE), F32), pltpu.VMEM((H_B * tq, 256), F32)])
    return pl.pallas_call(
        functools.partial(_nsa_kernel, tq, n_blocks, wt),
        out_shape=jax.ShapeDtypeStruct((nb, t, 256), BF16),
        grid_spec=grid_spec,
        compiler_params=_cparams(("arbitrary", "arbitrary")),
        name="nsa_prompt",
    )(qi, ki, qn, gt, kvc, selb, winb, lw['g_out'])


TOK_ROWS = D_MODEL // LANE


def _store_token_tiles(ref, x):
    n = x.shape[0]
    for j in range(TOK_ROWS):
        ref[pl.ds(j, n, stride=TOK_ROWS), :] = x[:, j * LANE:(j + 1) * LANE]


def _load_token_tiles(ref, n):
    return jnp.concatenate([ref[pl.ds(j, n, stride=TOK_ROWS), :] for j in range(TOK_ROWS)], axis=1)


def _outproj_kernel(ya_ref, yb_ref, yc_ref, yd_ref, x_ref, g1_ref, sc_ref, sh_ref, n2_ref, wo_ref, wr_ref, br_ref,
                    x1_ref, h2_ref, ti_ref, tw_ref):
    mix = jnp.zeros(x_ref.shape, F32)
    for g, y_ref in enumerate((ya_ref, yb_ref, yc_ref, yd_ref)):
        mix = mix + jnp.dot(y_ref[...].astype(BF16), wo_ref[g * GROUP_W:(g + 1) * GROUP_W, :],
                            preferred_element_type=F32)
    x1 = x_ref[...] + g1_ref[0] * mix
    x1_ref[...] = x1
    h2 = x1 * lax.rsqrt(jnp.mean(x1 * x1, -1, keepdims=True) + EPS) * n2_ref[...]
    h2 = h2 * (1.0 + sc_ref[0]) + sh_ref[0]
    _store_token_tiles(h2_ref, h2)
    logits = jnp.dot(h2, wr_ref[...], precision=HI, preferred_element_type=F32) + br_ref[...]
    lane = _lane_iota((1, LANE))
    ti = jnp.zeros(logits.shape, I32)
    tv = jnp.full(logits.shape, NEG_INF, F32)
    for k in range(TOP_K):
        m = jnp.max(logits, -1, keepdims=True)
        idx = jnp.min(jnp.where(logits == m, lane, LANE), -1, keepdims=True)
        ti = jnp.where(lane == k, idx, ti)
        tv = jnp.where(lane == k, m, tv)
        logits = jnp.where(lane == idx, -3e38, logits)
    e = jnp.where(lane < TOP_K, jnp.exp(tv - jnp.max(tv, -1, keepdims=True)), 0.0)
    ti_ref[...] = ti
    tw_ref[...] = e / jnp.sum(e, -1, keepdims=True)


def _outproj(ys, x, g1, sc2, sh2, lw, rows_per_mod):
    n, d = x.shape
    tm = min(512, n)
    per_row = rows_per_mod == 1
    if per_row:
        mod = pl.BlockSpec((1, tm, d), lambda i: (0, i, 0))
        g1, sc2, sh2 = [a.reshape(1, n, d) for a in (g1, sc2, sh2)]
    else:
        mod = pl.BlockSpec((1, 1, d), lambda i: (i * tm // rows_per_mod, 0, 0))
        g1, sc2, sh2 = [a.reshape(-1, 1, d) for a in (g1, sc2, sh2)]

    def row(w):
        return pl.BlockSpec((tm, w), lambda i: (i, 0))

    def const(shape):
        return pl.BlockSpec(shape, lambda i: (0, 0))

    return pl.pallas_call(
        _outproj_kernel,
        out_shape=[jax.ShapeDtypeStruct((n, d), F32), jax.ShapeDtypeStruct((n * TOK_ROWS, LANE), F32),
                   jax.ShapeDtypeStruct((n, LANE), I32), jax.ShapeDtypeStruct((n, LANE), F32)],
        grid=(n // tm,),
        in_specs=[row(256)] * 4 + [row(d), mod, mod, mod, const((1, d)), const((d, d)), const((d, LANE)),
                                    const((1, LANE))],
        out_specs=[row(d), pl.BlockSpec((tm * TOK_ROWS, LANE), lambda i: (i, 0)), row(LANE), row(LANE)],
        compiler_params=_cparams(("arbitrary",)),
        name="outproj",
    )(*ys, x, g1, sc2, sh2, lw['norm2'], lw['w_out'], lw['w_router'], lw['b_router'])


GATHER_CHUNK = 2048


def _gather_kernel(idx_ref, src_ref, dst_ref, sem):
    i = pl.program_id(0)

    def body(j, carry):
        r = idx_ref[0, 0, j]
        pltpu.make_async_copy(src_ref.at[pl.ds(pl.multiple_of(r * TOK_ROWS, TOK_ROWS), TOK_ROWS)],
                              dst_ref.at[pl.ds(pl.multiple_of((i * GATHER_CHUNK + j) * TOK_ROWS, TOK_ROWS), TOK_ROWS)],
                              sem).start()
        return carry

    lax.fori_loop(0, GATHER_CHUNK, body, 0, unroll=8)
    blk = GATHER_CHUNK * TOK_ROWS
    pltpu.make_async_copy(src_ref.at[pl.ds(0, blk)], dst_ref.at[pl.ds(pl.multiple_of(i * blk, blk), blk)],
                          sem).wait()


def _row_gather(src, idx):
    m = idx.shape[0]
    assert m % GATHER_CHUNK == 0
    nchunk = m // GATHER_CHUNK
    return pl.pallas_call(
        _gather_kernel,
        out_shape=jax.ShapeDtypeStruct((m * TOK_ROWS, LANE), src.dtype),
        grid=(nchunk,),
        in_specs=[pl.BlockSpec((1, 1, GATHER_CHUNK), lambda i: (i, 0, 0), memory_space=pltpu.SMEM),
                  pl.BlockSpec(memory_space=pl.ANY)],
        out_specs=pl.BlockSpec(memory_space=pl.ANY),
        scratch_shapes=[pltpu.SemaphoreType.DMA],
        compiler_params=_cparams(("arbitrary",)),
        name="row_gather",
    )(idx.reshape(nchunk, 1, GATHER_CHUNK), src)


MOE_TM = 512


def _moe_kernel(te_ref, tv_ref, xs_ref, wgu_ref, bgu_ref, wd_ref, bd_ref, o_ref, wgu_bf, wd_bf):
    i = pl.program_id(0)
    prev = te_ref[jnp.maximum(i - 1, 0)]

    @pl.when((i == 0) | (te_ref[i] != prev))
    def _():
        for c in range(8):
            wgu_bf[c * 128:(c + 1) * 128, :] = wgu_ref[0, c * 128:(c + 1) * 128, :].astype(BF16)
            wd_bf[c * 128:(c + 1) * 128, :] = wd_ref[0, c * 128:(c + 1) * 128, :].astype(BF16)

    @pl.when(tv_ref[i] == 1)
    def _():
        x = _load_token_tiles(xs_ref, MOE_TM).astype(BF16)
        hu = jnp.dot(x, wgu_bf[...], preferred_element_type=F32) + bgu_ref[0]
        gt = jnp.minimum(hu[:, :D_FF], SWIGLU_LIMIT)
        up = jnp.clip(hu[:, D_FF:], -SWIGLU_LIMIT, SWIGLU_LIMIT)
        act = gt * jax.nn.sigmoid(SWIGLU_ALPHA * gt) * (up + 1.0)
        _store_token_tiles(o_ref, jnp.dot(act.astype(BF16), wd_bf[...], preferred_element_type=F32) + bd_ref[0])

    @pl.when(tv_ref[i] == 0)
    def _():
        o_ref[...] = jnp.zeros_like(o_ref)


def _moe_ffn(xs, tile_expert, tile_valid, w_gu, b_gu, w_down, b_down):
    d = D_MODEL
    r = xs.shape[0] // TOK_ROWS
    n_tiles = r // MOE_TM
    tile_spec = pl.BlockSpec((MOE_TM * TOK_ROWS, LANE), lambda i, te, tv: (i, 0))
    grid_spec = pltpu.PrefetchScalarGridSpec(
        num_scalar_prefetch=2,
        grid=(n_tiles,),
        in_specs=[tile_spec,
                  pl.BlockSpec((1, d, 2 * D_FF), lambda i, te, tv: (te[i], 0, 0)),
                  pl.BlockSpec((1, 1, 2 * D_FF), lambda i, te, tv: (te[i], 0, 0)),
                  pl.BlockSpec((1, D_FF, d), lambda i, te, tv: (te[i], 0, 0)),
                  pl.BlockSpec((1, 1, d), lambda i, te, tv: (te[i], 0, 0))],
        out_specs=tile_spec,
        scratch_shapes=[pltpu.VMEM((d, 2 * D_FF), BF16), pltpu.VMEM((D_FF, d), BF16)])
    return pl.pallas_call(
        _moe_kernel,
        out_shape=jax.ShapeDtypeStruct((r * TOK_ROWS, LANE), F32),
        grid_spec=grid_spec,
        compiler_params=_cparams(("arbitrary",)),
        name="moe_ffn",
    )(tile_expert, tile_valid, xs, w_gu, b_gu.reshape(N_EXP, 1, 2 * D_FF), w_down, b_down.reshape(N_EXP, 1, d))


def _combine_kernel(x1_ref, y0_ref, y1_ref, y2_ref, y3_ref, tw_ref, g2_ref, o_ref):
    tw = tw_ref[...]
    n = x1_ref.shape[0]
    acc = jnp.zeros(x1_ref.shape, F32)
    for k, y_ref in enumerate((y0_ref, y1_ref, y2_ref, y3_ref)):
        acc = acc + tw[:, k:k + 1] * _load_token_tiles(y_ref.at[0], n)
    o_ref[...] = x1_ref[...] + g2_ref[0] * acc


def _combine(x1, y4, tw, g2, row_offset, rows_per_mod):
    n, d = x1.shape
    tm = min(512, n)
    off = row_offset // tm
    per_row = rows_per_mod == 1
    if per_row:
        mod = pl.BlockSpec((1, tm, d), lambda i: (0, i, 0))
        g2 = g2.reshape(1, n, d)
    else:
        mod = pl.BlockSpec((1, 1, d), lambda i: (i * tm // rows_per_mod, 0, 0))
        g2 = g2.reshape(-1, 1, d)
    yspecs = [pl.BlockSpec((1, tm * TOK_ROWS, LANE), functools.partial(lambda k, i: (k, i + off, 0), k))
              for k in range(TOP_K)]
    return pl.pallas_call(
        _combine_kernel,
        out_shape=jax.ShapeDtypeStruct((n, d), F32),
        grid=(n // tm,),
        in_specs=[pl.BlockSpec((tm, d), lambda i: (i, 0))] + yspecs +
                 [pl.BlockSpec((tm, LANE), lambda i: (i + off, 0)), mod],
        out_specs=pl.BlockSpec((tm, d), lambda i: (i, 0)),
        compiler_params=_cparams(("arbitrary",)),
        name="combine",
    )(x1, y4, y4, y4, y4, tw, g2)


def _moe(h2_all, ti_all, lw_moe):
    n = h2_all.shape[0] // TOK_ROWS
    pairs = n * TOP_K
    e = ti_all[:, :TOP_K].reshape(pairs)
    onehot = (e[:, None] == jnp.arange(N_EXP, dtype=I32)[None, :]).astype(I32)
    csum = jnp.cumsum(onehot, axis=0)
    rank = jnp.take_along_axis(csum, e[:, None], axis=1)[:, 0] - 1
    counts = csum[-1]
    padded = (counts + MOE_TM - 1) // MOE_TM * MOE_TM
    ends = jnp.cumsum(padded)
    starts = ends - padded
    row = starts[e] + rank
    r_max = -(-(pairs + N_EXP * MOE_TM) // GATHER_CHUNK) * GATHER_CHUNK
    tok_of_row = jnp.zeros((r_max,), I32).at[row].set(jnp.arange(pairs, dtype=I32) // TOP_K)
    tile_start = jnp.arange(r_max // MOE_TM, dtype=I32) * MOE_TM
    tile_valid = (tile_start < ends[-1]).astype(I32)
    tile_expert = jnp.minimum(jnp.sum((tile_start[:, None] >= ends[None, :]).astype(I32), axis=1), N_EXP - 1)
    tile_expert = jnp.where(tile_valid == 1, tile_expert, jnp.max(jnp.where(counts > 0, jnp.arange(N_EXP), 0)))
    xs = _row_gather(h2_all, tok_of_row)
    ys = _moe_ffn(xs, tile_expert, tile_valid, lw_moe['w_gu'], lw_moe['b_gu'], lw_moe['w_down'], lw_moe['b_down'])
    n_pad = -(-n // MOE_TM) * MOE_TM
    row_kmajor = jnp.pad(jnp.transpose(row.reshape(n, TOP_K)), ((0, 0), (0, n_pad - n))).reshape(TOP_K * n_pad)
    y4 = _row_gather(ys, row_kmajor)
    return y4.reshape(TOP_K, n_pad * TOK_ROWS, LANE)


DEC_PAGES = 16


def _rows_to_headmajor(o8, width):
    wide = jnp.concatenate([o8] * H_B, axis=1)
    row = lax.broadcasted_iota(I32, wide.shape, 0)
    lane = lax.broadcasted_iota(I32, wide.shape, 1)
    return jnp.sum(jnp.where(row == lane // width, wide, 0.0), axis=0, keepdims=True)


def _head_rows(q_row, width, pad_to):
    rows = [q_row[:, hh * width:(hh + 1) * width] for hh in range(H_B)]
    q4 = jnp.concatenate(rows + [jnp.zeros((8 - H_B, width), F32)], axis=0)
    if pad_to > width:
        q4 = jnp.concatenate([q4, jnp.zeros((8, pad_to - width), F32)], axis=1)
    return q4


def _slope_rows():
    row = lax.broadcasted_iota(I32, (8, 1), 0)
    out = jnp.zeros((8, 1), F32)
    for hh in range(H_B):
        out = jnp.where(row == hh, ALIBI[hh], out)
    return out


def _nt(a, b, **kw):
    return lax.dot_general(a, b, (((1,), (1,)), ((), ())), preferred_element_type=F32, **kw)


def _decode_stream_kernel(layer, n_pages, pg, pt_ref, qm_ref, kh_ref, cv_ref, gk_ref, wc_ref, wuvw_ref,
                          gout_ref, c_hbm, krt_hbm, cmp_hbm, ya_ref, ksum_ref,
                          cbuf, krbuf, cmpbuf, sem):
    b = pl.program_id(0)
    n_samples = pl.num_programs(0)
    n_chunks = n_pages // pg

    def copies(bb, chunk, slot):
        out = []
        for j in range(pg):
            page = pt_ref[bb, chunk * pg + j]
            rows = pl.ds(j * LANE, LANE)
            out.append(pltpu.make_async_copy(c_hbm.at[layer, page], cbuf.at[slot, rows], sem.at[0, slot]))
            out.append(pltpu.make_async_copy(krt_hbm.at[layer, page], krbuf.at[slot, j], sem.at[1, slot]))
            out.append(pltpu.make_async_copy(cmp_hbm.at[layer, page], cmpbuf.at[slot, rows], sem.at[2, slot]))
        return out

    @pl.when(b == 0)
    def _():
        for cp in copies(0, 0, 0):
            cp.start()

    q32 = qm_ref[0].astype(F32)
    qlat = jnp.concatenate([q32[:, hh * 256:hh * 256 + D_LAT] for hh in range(H_A)] +
                           [jnp.zeros((8 - H_A, D_LAT), F32)], axis=0)
    qrot = jnp.concatenate([q32[:, hh * 256 + D_LAT + hh * D_ROPE:hh * 256 + D_LAT + (hh + 1) * D_ROPE]
                            for hh in range(H_A)] + [jnp.zeros((8 - H_A, D_ROPE), F32)], axis=0)
    kh_new = kh_ref[0].astype(F32)
    m0 = (jnp.sum(qlat * kh_new[:, :D_LAT], -1, keepdims=True) +
          jnp.sum(qrot * kh_new[:, D_LAT:D_LAT + D_ROPE], -1, keepdims=True))
    l0 = jnp.ones((8, 1), F32)
    acc0 = jnp.broadcast_to(cv_ref[0][:, :D_LAT].astype(F32), (8, D_LAT))
    qlat_b, qrot_b = qlat.astype(BF16), qrot.astype(BF16)
    ones_c = jnp.ones((8, D_LAT), BF16)
    gk = gk_ref[...]
    wc = wc_ref[...]

    def body(chunk, carry):
        m_prev, l_prev, acc = carry
        slot = (b * n_chunks + chunk) % 2
        last = chunk + 1 == n_chunks
        nxt_b = jnp.where(last, b + 1, b)
        nxt_c = jnp.where(last, 0, chunk + 1)

        @pl.when(nxt_b < n_samples)
        def _():
            for cp in copies(nxt_b, nxt_c, 1 - slot):
                cp.start()

        for cp in copies(b, chunk, slot):
            cp.wait()
        c = cbuf[slot]
        krt = krbuf[slot]
        s_rot = jnp.concatenate([jnp.dot(qrot_b, krt[j].astype(BF16), preferred_element_type=F32)
                                 for j in range(pg)], axis=1)
        ss_rot = jnp.concatenate([jnp.sum(krt[j] * krt[j], axis=0, keepdims=True) for j in range(pg)], axis=1)
        ss = _nt(ones_c, (c * c).astype(BF16)) + ss_rot
        r = lax.rsqrt(ss * (1.0 / MLA_DIM) + EPS)
        s = (_nt(qlat_b, (c * gk).astype(BF16)) + s_rot) * r
        m_new = jnp.maximum(m_prev, jnp.max(s, -1, keepdims=True))
        p = jnp.exp(s - m_new)
        alpha = jnp.exp(m_prev - m_new)
        l_new = alpha * l_prev + jnp.sum(p, -1, keepdims=True)
        acc = alpha * acc + jnp.dot(p.astype(BF16), c.astype(BF16), preferred_element_type=F32)
        cm = cmpbuf[slot]
        ksum = jnp.sum(cm.reshape(2 * pg, BLK, LANE) * wc[None], axis=1)
        ksum_ref[0, pl.ds(pl.multiple_of(chunk * 2 * pg, 2 * pg), 2 * pg), :] = ksum
        return m_new, l_new, acc

    m_f, l_f, acc = lax.fori_loop(0, n_chunks, body, (m0, l0, acc0))
    o = acc / l_f
    yw = jnp.dot(o.astype(BF16), wuvw_ref[...], preferred_element_type=F32)
    row = lax.broadcasted_iota(I32, yw.shape, 0)
    lane2 = lax.broadcasted_iota(I32, yw.shape, 1)
    ya = jnp.sum(jnp.where(row == lane2 // D_VA, yw, 0.0), axis=0, keepdims=True)
    ya_ref[0] = ya * _group_rsqrt(ya, GROUP_W) * gout_ref[:, 0:256]


def _decode_stream(layer, page_table, qm, kh, cv, lw, cache_c, cache_krt, cache_cmp):
    db, n_pages = page_table.shape
    n_blk = 2 * n_pages
    pg = min(DEC_PAGES, n_pages)

    def per(w):
        return pl.BlockSpec((1, 1, w), lambda b, pt: (b, 0, 0))

    def const(shape):
        return pl.BlockSpec(shape, lambda b, pt: (0,) * len(shape))

    anyspec = pl.BlockSpec(memory_space=pl.ANY)
    grid_spec = pltpu.PrefetchScalarGridSpec(
        num_scalar_prefetch=1,
        grid=(db,),
        in_specs=[per(1024), per(256), per(256), const((1, LANE)), const((BLK, LANE)),
                  const((LANE, 256)), const((1, D_MODEL)), anyspec, anyspec, anyspec],
        out_specs=[per(256), pl.BlockSpec((1, n_blk, LANE), lambda b, pt: (b, 0, 0))],
        scratch_shapes=[pltpu.VMEM((2, pg * LANE, LANE), F32), pltpu.VMEM((2, pg, D_ROPE, LANE), F32),
                        pltpu.VMEM((2, pg * LANE, LANE), F32), pltpu.SemaphoreType.DMA((3, 2))])
    ya, ksum = pl.pallas_call(
        functools.partial(_decode_stream_kernel, layer, n_pages, pg),
        out_shape=[jax.ShapeDtypeStruct((db, 1, 256), F32), jax.ShapeDtypeStruct((db, n_blk, LANE), F32)],
        grid_spec=grid_spec,
        compiler_params=_cparams(("arbitrary",)),
        name="decode_stream",
    )(page_table, qm.reshape(db, 1, 1024), kh.reshape(db, 1, 256), cv.reshape(db, 1, 256),
      lw['g_k_mla'], lw['wc'], lw['wuv_wide'], lw['g_out'], cache_c, cache_krt, cache_cmp)
    return ya.reshape(db, 256), ksum


CMP_GROUP = 8


def _cmp_select_kernel(n_blk, past, ksum_ref, qn_ref, gcmp_ref, oc_ref, sel_ref):
    lane = _lane_iota((1, LANE))
    khalf = lane < D_NSA
    jb = lax.broadcasted_iota(I32, (1, n_blk), 1)
    cpos = (jb + 1) * BLK - 1
    bias = _slope_rows() * (past - cpos).astype(F32)
    ocs, imps = [], []
    for i in range(CMP_GROUP):
        ksum = ksum_ref[i]
        ssq = jnp.sum(jnp.where(khalf, ksum * ksum, 0.0), -1, keepdims=True)
        kvc = jnp.where(khalf, ksum * lax.rsqrt(ssq * (1.0 / D_NSA) + EPS) * gcmp_ref[...], ksum)
        q8 = _head_rows(qn_ref[i:i + 1, :], D_NSA, D_NSA)
        s_c = _nt(q8, kvc[:, :D_NSA], precision=HI) - bias
        e_c = jnp.exp(s_c - jnp.max(s_c, -1, keepdims=True))
        p_c = e_c / jnp.maximum(jnp.sum(e_c, -1, keepdims=True), 1e-30)
        oc8 = jnp.dot(p_c.astype(BF16), kvc[:, D_NSA:].astype(BF16), preferred_element_type=F32)
        ocs.append(_rows_to_headmajor(oc8, D_NSA))
        hrow = lax.broadcasted_iota(I32, p_c.shape, 0)
        imps.append(jnp.sum(jnp.where(hrow < H_B, p_c, 0.0), axis=0, keepdims=True))
    oc_ref[...] = jnp.concatenate(ocs, axis=0)
    imp = jnp.where(jb == 0, FORCE_SCORE, jnp.concatenate(imps, axis=0))
    sel = jnp.zeros((CMP_GROUP, LANE), I32)
    for k in range(N_SEL - 1):
        mx = jnp.max(imp, -1, keepdims=True)
        idx = jnp.min(jnp.where(imp == mx, jb, n_blk), -1, keepdims=True)
        sel = jnp.where(lane == k, idx, sel)
        imp = jnp.where(jb == idx, -3e38, imp)
    sel_ref[...] = sel


def _cmp_select(ksum, qn, lw, past):
    db, n_blk, _ = ksum.shape
    return pl.pallas_call(
        functools.partial(_cmp_select_kernel, n_blk, past),
        out_shape=[jax.ShapeDtypeStruct((db, 256), F32), jax.ShapeDtypeStruct((db, LANE), I32)],
        grid=(db // CMP_GROUP,),
        in_specs=[pl.BlockSpec((CMP_GROUP, n_blk, LANE), lambda g: (g, 0, 0)),
                  pl.BlockSpec((CMP_GROUP, 256), lambda g: (g, 0)),
                  pl.BlockSpec((1, LANE), lambda g: (0, 0))],
        out_specs=[pl.BlockSpec((CMP_GROUP, 256), lambda g: (g, 0)),
                   pl.BlockSpec((CMP_GROUP, LANE), lambda g: (g, 0))],
        compiler_params=_cparams(("arbitrary",)),
        name="cmp_select",
    )(ksum, qn, lw['g_k_cmp'])


def _decode_sel_kernel(layer, n_pages, w_buf, pt_ref, si_ref, qn_ref, gt_ref, oc_ref, snew_ref, wnew_ref, win_ref,
                       gout_ref, sel_hbm, yb_ref, selbuf, sem):
    b = pl.program_id(0)
    n_sel = N_SEL - 1
    past = n_pages * LANE

    def copies():
        out = []
        for r in range(n_sel):
            j = si_ref[b, r]
            page = pt_ref[b, j // 2]
            off = pl.multiple_of((j % 2) * BLK, BLK)
            out.append(pltpu.make_async_copy(sel_hbm.at[layer, page, pl.ds(off, BLK)],
                                             selbuf.at[pl.ds(r * BLK, BLK)], sem))
        return out

    for cp in copies():
        cp.start()
    q8 = _head_rows(qn_ref[0], D_NSA, LANE)
    q8b = q8.astype(BF16)
    q8r = q8b.astype(F32)
    slope = _slope_rows()
    lane = _lane_iota((1, LANE))

    def branch(kv, dist, valid, new_row):
        kvb = kv.astype(BF16)
        s = _nt(q8b, kvb) - slope * dist
        if valid is not None:
            s = jnp.where(valid, s, NEG_INF)
        newb = new_row.astype(BF16).astype(F32)
        s_new = jnp.sum(jnp.where(lane < D_NSA, q8r * newb, 0.0), -1, keepdims=True)
        m = jnp.maximum(jnp.max(s, -1, keepdims=True), s_new)
        e = jnp.exp(s - m)
        if valid is not None:
            e = jnp.where(valid, e, 0.0)
        e_new = jnp.exp(s_new - m)
        den = jnp.sum(e, -1, keepdims=True) + e_new
        o = (jnp.dot(e.astype(BF16), kvb, preferred_element_type=F32) + e_new * newb) / den
        return _rows_to_headmajor(pltpu.roll(o, D_NSA, 1)[:, :D_NSA], D_NSA)

    iw = lax.broadcasted_iota(I32, (1, w_buf), 1)
    distw = w_buf - iw
    validw = (distw < WINDOW) & (past - w_buf + iw >= 0)
    yw = branch(win_ref[0, 0], distw.astype(F32), validw, wnew_ref[0])

    for cp in copies():
        cp.wait()
    ls = lax.broadcasted_iota(I32, (1, n_sel * BLK), 1)
    spos = ls % BLK
    for r in range(n_sel):
        spos = spos + jnp.where(ls // BLK == r, si_ref[b, r] * BLK, 0)
    ys = branch(selbuf[...], (past - spos).astype(F32), None, snew_ref[0])
    g = gt_ref[0]
    y = _head_cols(g, 0) * oc_ref[0] + _head_cols(g, 1) * ys + _head_cols(g, 2) * yw
    yb_ref[0] = y * _group_rsqrt(y, GROUP_W) * gout_ref[:, 256:512]


def _decode_sel(layer, page_table, sel_idx, qn, gt, oc, sel_new, win_new, win_cache, lw, cache_sel):
    db, n_pages = page_table.shape
    w_buf = win_cache.shape[2]

    def per(w):
        return pl.BlockSpec((1, 1, w), lambda b, pt, si: (b, 0, 0))

    grid_spec = pltpu.PrefetchScalarGridSpec(
        num_scalar_prefetch=2,
        grid=(db,),
        in_specs=[per(256), per(LANE), per(256), per(LANE), per(LANE),
                  pl.BlockSpec((1, 1, w_buf, LANE), lambda b, pt, si: (layer, b, 0, 0)),
                  pl.BlockSpec((1, D_MODEL), lambda b, pt, si: (0, 0)),
                  pl.BlockSpec(memory_space=pl.ANY)],
        out_specs=per(256),
        scratch_shapes=[pltpu.VMEM(((N_SEL - 1) * BLK, LANE), F32), pltpu.SemaphoreType.DMA])
    yb = pl.pallas_call(
        functools.partial(_decode_sel_kernel, layer, n_pages, w_buf),
        out_shape=jax.ShapeDtypeStruct((db, 1, 256), F32),
        grid_spec=grid_spec,
        compiler_params=_cparams(("arbitrary",)),
        name="decode_sel",
    )(page_table, sel_idx, qn.reshape(db, 1, 256), gt.reshape(db, 1, LANE), oc.reshape(db, 1, 256),
      sel_new.reshape(db, 1, LANE),
      win_new.reshape(db, 1, LANE), win_cache, lw['g_out'], cache_sel)
    return yb.reshape(db, 256)


def kernel(x_prompt, x_sample, cache_mla_c, cache_mla_kr, cache_nsa_cmp_kv, cache_nsa_sel_kv, cache_nsa_win_kv,
           state_pool, page_table, c_prompt, c_sample, w_ada, b_ada, norm1, norm2, w_in, w_uk, w_uv, g_kv, g_q_mla,
           g_k_mla, g_q_nsa, g_k_cmp, g_k_sel, g_k_win, w_cmp_k, w_cmp_v, w_pool, pool_scale, g_sgu, w_s, b_s,
           g_out, w_out, w_router, b_router, w_gu, b_gu, w_down, b_down):
    w = dict(norm1=norm1, norm2=norm2, w_in=w_in, w_uk=w_uk, w_uv=w_uv, g_kv=g_kv, g_q_mla=g_q_mla, g_k_mla=g_k_mla,
             g_q_nsa=g_q_nsa, g_k_cmp=g_k_cmp, g_k_sel=g_k_sel, g_k_win=g_k_win, w_cmp_k=w_cmp_k, w_cmp_v=w_cmp_v,
             w_pool=w_pool, pool_scale=pool_scale, g_sgu=g_sgu, w_s=w_s, b_s=b_s, g_out=g_out, w_out=w_out,
             w_router=w_router, b_router=b_router, w_gu=w_gu, b_gu=b_gu, w_down=w_down, b_down=b_down)
    nb, t, d = x_prompt.shape
    db = x_sample.shape[0]
    depth = w_ada.shape[0]
    n_tok = nb * t
    past = page_table.shape[1] * cache_mla_c.shape[2]
    w_keep = min(WINDOW, t)

    rows = -(-(nb + db) // 8) * 8
    c_all = jnp.concatenate([c_prompt, c_sample, jnp.zeros((rows - nb - db, d), F32)], axis=0)
    ada = _ada(c_all, w_ada, b_ada)
    cache_krt = jnp.swapaxes(cache_mla_kr, 2, 3)

    xp = x_prompt
    xs = x_sample.reshape(1, db, d)
    st = [[] for _ in range(13)]
    for l in range(depth):
        lw = _layer_weights(l, w)
        sh1p, sc1p, g1p, sh2p, sc2p, g2p = [z[:, None, :] for z in jnp.split(ada[l, :nb], 6, axis=-1)]
        sh1s, sc1s, g1s, sh2s, sc2s, g2s = jnp.split(ada[l, nb:nb + db], 6, axis=-1)

        o = _inproj(xp, sh1p, sc1p, lw, 0)
        ya = _mla_prompt(o['qm'], o['kh'], o['cv'], lw)
        yb = _nsa_prompt(o['qn'], o['gt'], o['kvc'], o['selb'], o['winb'], lw)
        ys = [y.reshape(n_tok, GROUP_W) for y in (ya, yb, o['yc'], o['yd'])]
        x1p, h2p, tip, twp = _outproj(ys, xp.reshape(n_tok, d), g1p, sc2p, sh2p, lw, t)

        s = _inproj(xs, sh1s[None], sc1s[None], lw, past, pool_buf=state_pool[l])
        ya_s, ksum_s = _decode_stream(l, page_table, s['qm'][0], s['kh'][0], s['cv'][0], lw,
                                      cache_mla_c, cache_krt, cache_nsa_cmp_kv)
        oc_s, sel_s = _cmp_select(ksum_s, s['qn'][0], lw, past)
        yb_s = _decode_sel(l, page_table, sel_s, s['qn'][0], s['gt'][0], oc_s, s['sel'][0],
                           s['win'][0], cache_nsa_win_kv, lw, cache_nsa_sel_kv)
        x1s, h2s, tis, tws = _outproj([ya_s, yb_s, s['yc'][0], s['yd'][0]], xs[0], g1s, sc2s, sh2s, lw, 1)

        y4 = _moe(jnp.concatenate([h2p, h2s], axis=0), jnp.concatenate([tip, tis], axis=0), lw['moe'])
        tw_all = jnp.concatenate([twp, tws], axis=0)
        xp = _combine(x1p, y4, tw_all, g2p, 0, t).reshape(nb, t, d)
        xs = _combine(x1s, y4, tw_all, g2s, n_tok, 1).reshape(1, db, d)

        per_l = (o['cl'], s['cl'][0][:, None, :], o['kr'], s['kr'][0][:, None, :], o['cmp'], s['cmp'][0][:, None, :],
                 o['sel'], s['sel'][0][:, None, :], o['win'][:, t - w_keep:],
                 jnp.concatenate([cache_nsa_win_kv[l][:, 1:], s['win'][0][:, None, :]], axis=1),
                 o['zc_tail'][:, 16 - POOL_PAD:],
                 jnp.concatenate([state_pool[l][:, 1:], s['zc_tail'][0][:, None, :]], axis=1),
                 s['v_tail'][0][:, None, :])
        for k, v in enumerate(per_l):
            st[k].append(v)
    return (xp, xs.reshape(db, 1, d)) + tuple(jnp.stack(v, axis=0) for v in st)
```

```python
import functools

import numpy as np
import jax
import jax.numpy as jnp
from jax import lax
from jax.experimental import pallas as pl
from jax.experimental.pallas import tpu as pltpu

F32 = jnp.float32
BF16 = jnp.bfloat16
I32 = jnp.int32
HI = lax.Precision.HIGHEST

EPS = 1e-6
NEG_INF = -1e30
D_MODEL = 1024
GROUP_W = 256
H_A, D_NOPE, D_ROPE, D_LAT, D_VA = 4, 64, 32, 128, 64
MLA_DIM = D_LAT + D_ROPE
MLA_SCALE = (D_NOPE + D_ROPE) ** -0.5
ROPE_BASE = 10000.0
H_B, D_NSA, BLK, N_SEL, WINDOW = 4, 64, 64, 16, 512
NSA_SCALE = D_NSA ** -0.5
FORCE_SCORE = 1e4
POOL_WINDOWS = (2, 4, 8, 16)
POOL_PAD = 15
H_D, CHUNK, D_VD = 4, 128, 64
N_EXP, TOP_K, D_FF = 32, 4, 1024
SWIGLU_LIMIT, SWIGLU_ALPHA = 7.0, 1.702
ALIBI = tuple(2.0 ** (-8.0 * (i + 1) / H_B) for i in range(H_B))

LANE = 128
VMEM_LIMIT = 56 * 1024 * 1024

C_QN, C_QR, C_QRS, C_CA, C_KR, C_KRS, C_QB, C_CMP, C_SEL, C_WIN, C_ZG, C_ZC, C_ZD, N_PAD = (
    0, 256, 384, 512, 640, 768, 896, 1152, 1280, 1408, 1536, 1664, 1920, 2432)


def _in_cols():
    qa, ca, ra, qb = 0, 384, 512, 544
    cmp_, sel, win, zg, zc, zd, n_in = 800, 928, 1056, 1184, 1196, 1452, 1964
    per = D_NOPE + D_ROPE
    half = D_ROPE // 2
    cols = []
    for h in range(H_A):
        cols += list(range(qa + h * per, qa + h * per + D_NOPE))
    for h in range(H_A):
        cols += list(range(qa + h * per + D_NOPE, qa + (h + 1) * per))
    for h in range(H_A):
        b = qa + h * per + D_NOPE
        cols += list(range(b + half, b + D_ROPE)) + list(range(b, b + half))
    cols += list(range(ca, ca + D_LAT))
    cols += list(range(ra, ra + D_ROPE)) * 4
    cols += (list(range(ra + half, ra + D_ROPE)) + list(range(ra, ra + half))) * 4
    cols += list(range(qb, qb + 256))
    cols += list(range(cmp_, cmp_ + 128)) + list(range(sel, sel + 128)) + list(range(win, win + 128))
    cols += list(range(zg, zg + 12)) + [n_in] * (LANE - 12)
    cols += list(range(zc, zc + 256))
    cols += list(range(zd, zd + 512))
    assert len(cols) == N_PAD
    return np.asarray(cols, np.int32)


_IN_COLS = _in_cols()


def _cparams(sem):
    return pltpu.CompilerParams(dimension_semantics=sem, vmem_limit_bytes=VMEM_LIMIT)


def _lane_iota(shape):
    return lax.broadcasted_iota(I32, shape, len(shape) - 1)


def _group_rsqrt(x, width):
    n = x.shape[-1]
    sq = x * x
    if width == n:
        return lax.rsqrt(jnp.sum(sq, -1, keepdims=True) * (1.0 / width) + EPS)
    lane = _lane_iota((1, n))
    out = jnp.zeros_like(x)
    for g in range(n // width):
        m = (lane >= g * width) & (lane < (g + 1) * width)
        ss = jnp.sum(jnp.where(m, sq, 0.0), -1, keepdims=True)
        out = jnp.where(m, lax.rsqrt(ss * (1.0 / width) + EPS), out)
    return out


def _ada_kernel(c_ref, w_ref, b_ref, o_ref):
    c = c_ref[...]
    s = c * jax.nn.sigmoid(c)
    o_ref[0] = jnp.dot(s.astype(BF16), w_ref[0].astype(BF16), preferred_element_type=F32) + b_ref[0]


def _ada(c_all, w_ada, b_ada):
    depth, d, n = w_ada.shape
    rows = c_all.shape[0]
    tn = 1536
    return pl.pallas_call(
        _ada_kernel,
        out_shape=jax.ShapeDtypeStruct((depth, rows, n), F32),
        grid=(depth, n // tn),
        in_specs=[pl.BlockSpec((rows, d), lambda l, j: (0, 0)),
                  pl.BlockSpec((1, d, tn), lambda l, j: (l, 0, j)),
                  pl.BlockSpec((1, 1, tn), lambda l, j: (l, 0, j))],
        out_specs=pl.BlockSpec((1, rows, tn), lambda l, j: (l, 0, j)),
        compiler_params=_cparams(("arbitrary", "arbitrary")),
        name="ada",
    )(c_all, w_ada, b_ada.reshape(depth, 1, n))


def _gelu_tanh(x):
    return 0.5 * x * (1.0 + jnp.tanh(0.7978845608028654 * (x + 0.044715 * (x * x * x))))


def _inproj_kernel(sample, tm, x_ref, sh_ref, sc_ref, n1_ref, w_ref, cos_ref, sin_ref, wuk_ref, gq_ref,
                   gkv_ref, gk_ref, gqn_ref, gcmp_ref, gsel_ref, gwin_ref, wc_ref, wpool_ref, pscale_ref,
                   gsgu_ref, ws_ref, bs_ref, gout_ref, pool_ref,
                   qm_ref, kh_ref, cl_ref, cv_ref, kr_ref, qn_ref, gt_ref, cmp_ref, sel_ref, selb_ref,
                   win_ref, winb_ref, kvc_ref, zc_ref, v_ref, yc_ref, yd_ref, carry_ref):
    i = pl.program_id(1)
    x = x_ref[0]
    xn = x * lax.rsqrt(jnp.mean(x * x, -1, keepdims=True) + EPS) * n1_ref[...]
    h = xn * (1.0 + sc_ref[0]) + sh_ref[0]
    z = jnp.dot(h.astype(BF16), w_ref[...], preferred_element_type=F32)
    cos4, sin4 = cos_ref[...], sin_ref[...]
    lane = _lane_iota((1, LANE))

    qlat = jnp.dot(z[:, C_QN:C_QN + 256].astype(BF16), wuk_ref[...], preferred_element_type=F32)
    qrot = z[:, C_QR:C_QR + LANE] * cos4 + z[:, C_QRS:C_QRS + LANE] * sin4
    for hh in range(H_A):
        lat = qlat[:, hh * D_LAT:(hh + 1) * D_LAT]
        rot = jnp.where((lane >= hh * D_ROPE) & (lane < (hh + 1) * D_ROPE), qrot, 0.0)
        ss = jnp.sum(lat * lat, -1, keepdims=True) + jnp.sum(rot * rot, -1, keepdims=True)
        r = lax.rsqrt(ss * (1.0 / MLA_DIM) + EPS) * MLA_SCALE
        qm_ref[0, :, hh * 256:hh * 256 + D_LAT] = (lat * r * gq_ref[:, hh * D_LAT:(hh + 1) * D_LAT]).astype(BF16)
        qm_ref[0, :, hh * 256 + D_LAT:(hh + 1) * 256] = (rot * r).astype(BF16)

    ca = z[:, C_CA:C_CA + LANE]
    c_lat = ca * _group_rsqrt(ca, LANE) * gkv_ref[...]
    kr4 = z[:, C_KR:C_KR + LANE] * cos4 + z[:, C_KRS:C_KRS + LANE] * sin4
    cl_ref[0] = c_lat
    ones_blk = jnp.ones((tm, LANE), BF16)
    cv_ref[0, :, :LANE] = c_lat.astype(BF16)
    cv_ref[0, :, LANE:] = ones_blk
    kr_ref[0] = kr4[:, :D_ROPE]
    ssk = jnp.sum(c_lat * c_lat, -1, keepdims=True) + 0.25 * jnp.sum(kr4 * kr4, -1, keepdims=True)
    rk = lax.rsqrt(ssk * (1.0 / MLA_DIM) + EPS)
    kh_ref[0, :, :LANE] = (c_lat * rk * gk_ref[...]).astype(BF16)
    kh_ref[0, :, LANE:] = (kr4 * rk).astype(BF16)

    zqb = z[:, C_QB:C_QB + 256]
    qn_ref[0] = zqb * _group_rsqrt(zqb, D_NSA) * gqn_ref[...] * NSA_SCALE
    gt_ref[0] = jax.nn.sigmoid(z[:, C_ZG:C_ZG + LANE])
    cmp_rows = z[:, C_CMP:C_CMP + LANE]
    cmp_ref[0] = cmp_rows
    khalf = lane < D_NSA

    def norm_k(zz, g_ref):
        ss = jnp.sum(jnp.where(khalf, zz * zz, 0.0), -1, keepdims=True)
        return jnp.where(khalf, zz * lax.rsqrt(ss * (1.0 / D_NSA) + EPS) * g_ref[...], zz)

    sel_rows = norm_k(z[:, C_SEL:C_SEL + LANE], gsel_ref)
    win_rows = norm_k(z[:, C_WIN:C_WIN + LANE], gwin_ref)
    sel_ref[0] = sel_rows
    selb_ref[0, :, :LANE] = sel_rows.astype(BF16)
    selb_ref[0, :, LANE:] = ones_blk
    win_ref[0] = win_rows
    winb_ref[0, :, :LANE] = win_rows.astype(BF16)
    winb_ref[0, :, LANE:] = ones_blk

    zc = z[:, C_ZC:C_ZC + 256]
    zd = z[:, C_ZD:C_ZD + 512]
    uv = _gelu_tanh(zd)
    u = uv[:, :256]
    v = uv[:, 256:]
    v = v * _group_rsqrt(v, D_VD) * gsgu_ref[...]
    lane2 = _lane_iota((1, 256))
    wsel = [lane2 < 64, lane2 < 128, lane2 < 192]

    def pick(a2, a4, a8, a16):
        return jnp.where(wsel[0], a2, jnp.where(wsel[1], a4, jnp.where(wsel[2], a8, a16)))

    wlane = pick(2.0, 4.0, 8.0, 16.0)
    if not sample:
        nbk = tm // BLK
        ksum = jnp.sum(cmp_rows.reshape(nbk, BLK, LANE) * wc_ref[...][None], axis=1)
        kvc_ref[0] = norm_k(ksum, gcmp_ref)

        @pl.when(i == 0)
        def _():
            carry_ref[...] = jnp.zeros_like(carry_ref)

        ext = jnp.concatenate([carry_ref[...], zc], axis=0)
        a2 = ext + pltpu.roll(ext, 1, 0)
        a4 = a2 + pltpu.roll(a2, 2, 0)
        a8 = a4 + pltpu.roll(a4, 4, 0)
        a16 = a8 + pltpu.roll(a8, 8, 0)
        wsum = pick(a2, a4, a8, a16)[16:]
        carry_ref[...] = zc[tm - 16:]
        pos = (i * tm + lax.broadcasted_iota(I32, (tm, 1), 0)).astype(F32)
        cnt = jnp.minimum(pos + 1.0, wlane)
        zc_ref[0] = zc[tm - 16:]
        v_ref[0] = v[tm - 16:]
        for c in range(tm // CHUNK):
            vc = v[c * CHUNK:(c + 1) * CHUNK]
            vst = jnp.concatenate(
                [jnp.where((lane2 >= hh * D_VD) & (lane2 < (hh + 1) * D_VD), vc, 0.0) for hh in range(H_D)], axis=0)
            mixed = jnp.dot(ws_ref[...], vst.astype(BF16), preferred_element_type=F32) + bs_ref[...]
            yd = u[c * CHUNK:(c + 1) * CHUNK] * mixed
            yd_ref[0, c * CHUNK:(c + 1) * CHUNK, :] = (
                yd * _group_rsqrt(yd, GROUP_W) * gout_ref[:, 768:1024]).astype(BF16)
    else:
        kvc_ref[0] = jnp.zeros_like(kvc_ref[0])
        pb = pool_ref[...]
        sums = []
        for w in POOL_WINDOWS:
            sums.append(zc + jnp.sum(pb[:, POOL_PAD - (w - 1):, :], axis=1))
        wsum = pick(*sums)
        cnt = wlane
        zc_ref[0] = zc
        v_ref[0] = v
        mixed = ws_ref[...] * v + bs_ref[...]
        yd = u * mixed
        yd_ref[0] = (yd * _group_rsqrt(yd, GROUP_W) * gout_ref[:, 768:1024]).astype(BF16)
    d = wsum / cnt - zc
    yc = jnp.dot(d.astype(BF16), wpool_ref[...], preferred_element_type=F32) * pscale_ref[...]
    yc_ref[0] = (yc * _group_rsqrt(yc, GROUP_W) * gout_ref[:, 512:768]).astype(BF16)


def _inproj(x, sh1, sc1, lw, pos0, pool_buf=None):
    sample = pool_buf is not None
    nb, t, d = x.shape
    tm = t if sample else min(512, t)
    nt = t // tm
    half = D_ROPE // 2
    freqs = ROPE_BASE ** (-jnp.arange(half, dtype=F32) / half)
    pos = (jnp.full((t,), pos0, I32) if sample else pos0 + jnp.arange(t, dtype=I32)).astype(F32)
    ang = pos[:, None] * freqs[None, :]
    cos, sin = jnp.cos(ang), jnp.sin(ang)
    cos4 = jnp.tile(jnp.concatenate([cos, cos], -1), (1, 4))
    sin4 = jnp.tile(jnp.concatenate([-sin, sin], -1), (1, 4))
    if sample:
        ws = jnp.repeat(lw['w_s'][:, 0, 0], D_VD)[None, :]
        bs = jnp.repeat(lw['b_s'][:, 0], D_VD)[None, :]
        pool = pool_buf
        pool_spec = pl.BlockSpec(pool.shape, lambda b, i: (0, 0, 0))
        ws_spec = pl.BlockSpec((1, 256), lambda b, i: (0, 0))
        bs_spec = pl.BlockSpec((1, 256), lambda b, i: (0, 0))
        tail = tm
    else:
        ws, bs = lw['ws_cat'], lw['bs_full']
        pool = jnp.zeros((1, 8, LANE), F32)
        pool_spec = pl.BlockSpec(pool.shape, lambda b, i: (0, 0, 0))
        ws_spec = pl.BlockSpec((CHUNK, 4 * CHUNK), lambda b, i: (0, 0))
        bs_spec = pl.BlockSpec((CHUNK, 256), lambda b, i: (0, 0))
        tail = 16
    nbk = max(tm // BLK, 8)

    def row(w):
        return pl.BlockSpec((1, tm, w), lambda b, i: (b, i, 0))

    def const(shape):
        return pl.BlockSpec(shape, lambda b, i: (0,) * len(shape))

    def tailspec(w):
        return pl.BlockSpec((1, tail, w), lambda b, i: (b, 0, 0))

    mod = row(d) if sample else pl.BlockSpec((1, 1, d), lambda b, i: (b, 0, 0))
    in_specs = [row(d), mod, mod,
                const((1, d)), const((d, N_PAD)),
                pl.BlockSpec((tm, LANE), lambda b, i: (i, 0)), pl.BlockSpec((tm, LANE), lambda b, i: (i, 0)),
                const((256, 512)), const((1, 512)), const((1, LANE)), const((1, LANE)), const((1, 256)),
                const((1, LANE)), const((1, LANE)), const((1, LANE)), const((BLK, LANE)), const((256, 256)),
                const((1, 256)), const((1, 256)), ws_spec, bs_spec, const((1, d)), pool_spec]
    outs = [('qm', 1024, BF16), ('kh', 256, BF16), ('cl', LANE, F32), ('cv', 256, BF16), ('kr', D_ROPE, F32),
            ('qn', 256, F32), ('gt', LANE, F32), ('cmp', LANE, F32), ('sel', LANE, F32), ('selb', 256, BF16),
            ('win', LANE, F32), ('winb', 256, BF16)]
    out_shape = [jax.ShapeDtypeStruct((nb, t, w), dt) for _, w, dt in outs]
    out_specs = [row(w) for _, w, _ in outs]
    out_shape += [jax.ShapeDtypeStruct((nb, nt * nbk, LANE), F32)]
    out_specs += [pl.BlockSpec((1, nbk, LANE), lambda b, i: (b, i, 0))]
    out_shape += [jax.ShapeDtypeStruct((nb, tail, 256), F32), jax.ShapeDtypeStruct((nb, tail, 256), F32)]
    out_specs += [tailspec(256), tailspec(256)]
    out_shape += [jax.ShapeDtypeStruct((nb, t, 256), BF16), jax.ShapeDtypeStruct((nb, t, 256), BF16)]
    out_specs += [row(256), row(256)]
    res = pl.pallas_call(
        functools.partial(_inproj_kernel, sample, tm),
        out_shape=out_shape,
        grid=(nb, nt),
        in_specs=in_specs,
        out_specs=out_specs,
        scratch_shapes=[pltpu.VMEM((16, 256), F32)],
        compiler_params=_cparams(("arbitrary", "arbitrary")),
        name="inproj_sample" if sample else "inproj_prompt",
    )(x, sh1, sc1, lw['norm1'], lw['w_pad'], cos4, sin4, lw['wuk_bd'], lw['gq_mla'], lw['g_kv'], lw['g_k_mla'],
      lw['gq_nsa'], lw['g_k_cmp'], lw['g_k_sel'], lw['g_k_win'], lw['wc'], lw['wpool_bd'], lw['pool_scale'],
      lw['g_sgu'], ws, bs, lw['g_out'], pool)
    names = [n for n, _, _ in outs] + ['kvc', 'zc_tail', 'v_tail', 'yc', 'yd']
    return dict(zip(names, res))


def _block_diag(blocks):
    n = len(blocks)
    r, c = blocks[0].shape
    out = jnp.zeros((n * r, n * c), blocks[0].dtype)
    for k, b in enumerate(blocks):
        out = out.at[k * r:(k + 1) * r, k * c:(k + 1) * c].set(b)
    return out


def _pad_lanes(v, n=LANE):
    return jnp.concatenate([v, jnp.ones((n - v.shape[0],), v.dtype)])[None, :]


def _layer_weights(l, w):
    lw = {}
    w_in = jnp.concatenate([w['w_in'][l], jnp.zeros((D_MODEL, 1), F32)], axis=1)
    lw['w_pad'] = jnp.take(w_in, _IN_COLS, axis=1).astype(BF16)
    lw['norm1'] = w['norm1'][l][None, :]
    lw['norm2'] = w['norm2'][l][None, :]
    lw['wuk_bd'] = _block_diag([w['w_uk'][l][h] for h in range(H_A)]).astype(BF16)
    lw['wuv_bd'] = _block_diag([w['w_uv'][l][h] for h in range(H_A)]).astype(BF16)
    lw['wuv_wide'] = jnp.concatenate([w['w_uv'][l][h] for h in range(H_A)], axis=1).astype(BF16)
    lw['gq_mla'] = w['g_q_mla'][l].reshape(1, H_A * D_LAT)
    lw['g_kv'] = w['g_kv'][l][None, :]
    lw['g_k_mla'] = w['g_k_mla'][l][None, :]
    lw['gq_nsa'] = jnp.tile(w['g_q_nsa'][l], H_B)[None, :]
    lw['g_k_cmp'] = _pad_lanes(w['g_k_cmp'][l])
    lw['g_k_sel'] = _pad_lanes(w['g_k_sel'][l])
    lw['g_k_win'] = _pad_lanes(w['g_k_win'][l])
    lw['wc'] = jnp.concatenate([jnp.tile(w['w_cmp_k'][l][:, None], (1, D_NSA)),
                                jnp.tile(w['w_cmp_v'][l][:, None], (1, D_NSA))], axis=1)
    lw['wpool_bd'] = _block_diag([w['w_pool'][l][g] for g in range(4)]).astype(BF16)
    lw['pool_scale'] = w['pool_scale'][l][None, :]
    lw['g_sgu'] = w['g_sgu'][l][None, :]
    tril = jnp.tril(jnp.ones((CHUNK, CHUNK), F32))
    lw['ws_cat'] = jnp.concatenate([w['w_s'][l][h] * tril for h in range(H_D)], axis=1).astype(BF16)
    lw['bs_full'] = jnp.repeat(jnp.transpose(w['b_s'][l]), D_VD, axis=1)
    lw['w_s'] = w['w_s'][l]
    lw['b_s'] = w['b_s'][l]
    lw['g_out'] = w['g_out'][l][None, :]
    lw['w_out'] = w['w_out'][l].astype(BF16)
    lw['w_router'] = jnp.pad(w['w_router'][l], ((0, 0), (0, LANE - N_EXP)))
    lw['b_router'] = jnp.concatenate([w['b_router'][l], jnp.full((LANE - N_EXP,), NEG_INF, F32)])[None, :]
    lw['moe'] = dict(layer=l, w_gu=w['w_gu'], b_gu=w['b_gu'], w_down=w['w_down'], b_down=w['b_down'])
    return lw


def _causal_pairs(nq):
    qi = np.concatenate([np.full((q + 1,), q, np.int32) for q in range(nq)])
    ki = np.concatenate([np.arange(q + 1, dtype=np.int32) for q in range(nq)])
    return jnp.asarray(qi), jnp.asarray(ki)


def _flash_step(s, v_ext, m_ref, acc_ref):
    m_prev = m_ref[...]
    m_new = jnp.maximum(m_prev, jnp.max(s, -1, keepdims=True))
    m_use = jnp.maximum(m_new, 0.1 * NEG_INF)
    p = jnp.exp(s - jnp.concatenate([m_use] * (s.shape[1] // LANE), axis=1))
    alpha = jnp.exp(m_prev - m_new)
    acc_ref[...] = (jnp.concatenate([alpha, alpha], axis=1) * acc_ref[...] +
                    jnp.dot(p.astype(BF16), v_ext, preferred_element_type=F32))
    m_ref[...] = m_new


def _mla_kernel(tq, qi_ref, ki_ref, q_ref, k_ref, v_ref, wuv_ref, gout_ref, o_ref, q4_ref, m_ref, acc_ref):
    p_id = pl.program_id(1)
    qi, ki = qi_ref[p_id], ki_ref[p_id]

    @pl.when(ki == 0)
    def _():
        for hh in range(H_A):
            q4_ref[hh * tq:(hh + 1) * tq, :] = q_ref[0, :, hh * 256:(hh + 1) * 256]
        m_ref[...] = jnp.full_like(m_ref, NEG_INF)
        acc_ref[...] = jnp.zeros_like(acc_ref)

    def update(masked):
        s = lax.dot_general(q4_ref[...], k_ref[0], (((1,), (1,)), ((), ())), preferred_element_type=F32)
        if masked:
            row = lax.broadcasted_iota(I32, (tq, tq), 0)
            col = lax.broadcasted_iota(I32, (tq, tq), 1)
            keep = jnp.concatenate([col <= row] * H_A, axis=0)
            s = jnp.where(keep, s, NEG_INF)
        _flash_step(s, v_ref[0], m_ref, acc_ref)

    @pl.when(ki < qi)
    def _():
        update(False)

    @pl.when(ki == qi)
    def _():
        update(True)
        acc = acc_ref[...]
        o = acc[:, :LANE] / acc[:, LANE:]
        ocat = jnp.concatenate([o[hh * tq:(hh + 1) * tq] for hh in range(H_A)], axis=1)
        ya = jnp.dot(ocat.astype(BF16), wuv_ref[...], preferred_element_type=F32)
        o_ref[0] = (ya * _group_rsqrt(ya, GROUP_W) * gout_ref[:, 0:256]).astype(BF16)


def _mla_prompt(qm, kh, cv, lw):
    nb, t, _ = qm.shape
    tq = min(512, t)
    nq = t // tq
    qi, ki = _causal_pairs(nq)
    grid_spec = pltpu.PrefetchScalarGridSpec(
        num_scalar_prefetch=2,
        grid=(nb, qi.shape[0]),
        in_specs=[pl.BlockSpec((1, tq, 1024), lambda b, p, qi, ki: (b, qi[p], 0)),
                  pl.BlockSpec((1, tq, 256), lambda b, p, qi, ki: (b, ki[p], 0)),
                  pl.BlockSpec((1, tq, 256), lambda b, p, qi, ki: (b, ki[p], 0)),
                  pl.BlockSpec((512, 256), lambda b, p, qi, ki: (0, 0)),
                  pl.BlockSpec((1, D_MODEL), lambda b, p, qi, ki: (0, 0))],
        out_specs=pl.BlockSpec((1, tq, 256), lambda b, p, qi, ki: (b, qi[p], 0)),
        scratch_shapes=[pltpu.VMEM((H_A * tq, 256), BF16), pltpu.VMEM((H_A * tq, LANE), F32),
                        pltpu.VMEM((H_A * tq, 256), F32)])
    return pl.pallas_call(
        functools.partial(_mla_kernel, tq),
        out_shape=jax.ShapeDtypeStruct((nb, t, 256), BF16),
        grid_spec=grid_spec,
        compiler_params=_cparams(("arbitrary", "arbitrary")),
        name="mla_prompt",
    )(qi, ki, qm, kh, cv, lw['wuv_bd'], lw['g_out'])


def _topk_mask(imp, jb, n_blocks, k):
    sel = jnp.zeros(imp.shape, F32)
    for _ in range(k):
        m = jnp.max(imp, -1, keepdims=True)
        idx = jnp.min(jnp.where(imp == m, jb, n_blocks), -1, keepdims=True)
        pick = jb == idx
        sel = jnp.where(pick, 1.0, sel)
        imp = jnp.where(pick, -3e38, imp)
    return sel


def _masked_softmax(s, keep):
    s = jnp.where(keep, s, NEG_INF)
    m = jnp.max(s, -1, keepdims=True)
    e = jnp.where(keep, jnp.exp(s - m), 0.0)
    return e / jnp.maximum(jnp.sum(e, -1, keepdims=True), 1e-30)


def _merge_heads(parts):
    lane = _lane_iota((1, LANE))
    lo = lane < D_NSA
    pair = [jnp.where(lo, pltpu.roll(parts[2 * j], D_NSA, 1), parts[2 * j + 1]) for j in range(2)]
    return jnp.concatenate(pair, axis=1)


def _head_cols(g, offset):
    lane2 = _lane_iota((1, 256))
    out = jnp.zeros((g.shape[0], 256), F32)
    for hh in range(H_B):
        col = g[:, 3 * hh + offset:3 * hh + offset + 1]
        out = jnp.where((lane2 >= hh * D_NSA) & (lane2 < (hh + 1) * D_NSA), col, out)
    return out


def _nsa_kernel(tq, n_blocks, wt, qi_ref, ki_ref, qn_ref, gt_ref, kvc_ref, selb_ref, winb_ref, gout_ref, o_ref,
                q4_ref, selm_ref, oc_ref, ms_ref, as_ref, mw_ref, aw_ref):
    p_id = pl.program_id(1)
    qi, ki = qi_ref[p_id], ki_ref[p_id]
    qpos = qi * tq + lax.broadcasted_iota(I32, (tq, 1), 0)

    @pl.when(ki == 0)
    def _():
        q = qn_ref[0]
        kvc = kvc_ref[0]
        kc = kvc[:, :D_NSA]
        vc = kvc[:, D_NSA:].astype(BF16)
        jb = lax.broadcasted_iota(I32, (1, n_blocks), 1)
        cpos = (jb + 1) * BLK - 1
        valid = cpos <= qpos
        dist = (qpos - cpos).astype(F32)
        imp = jnp.zeros((tq, n_blocks), F32)
        for hh in range(H_B):
            qh = q[:, hh * D_NSA:(hh + 1) * D_NSA]
            s = lax.dot_general(qh, kc, (((1,), (1,)), ((), ())), precision=HI, preferred_element_type=F32)
            pc = _masked_softmax(s - ALIBI[hh] * dist, valid)
            oc_ref[:, hh * D_NSA:(hh + 1) * D_NSA] = jnp.dot(pc.astype(BF16), vc, preferred_element_type=F32)
            imp = imp + pc
            q4_ref[hh * tq:(hh + 1) * tq, :] = jnp.concatenate(
                [qh, jnp.zeros((tq, LANE - D_NSA), F32)], axis=1).astype(BF16)
        cur = qpos // BLK
        imp = jnp.where((jb == 0) | (jb == cur), FORCE_SCORE, jnp.where(jb > cur, -1.0, imp))
        selm_ref[...] = _topk_mask(imp, jb, n_blocks, min(N_SEL, n_blocks)).astype(BF16)
        for m_r, a_r in ((ms_ref, as_ref), (mw_ref, aw_ref)):
            m_r[...] = jnp.full_like(m_r, NEG_INF)
            a_r[...] = jnp.zeros_like(a_r)

    kpos = ki * tq + lax.broadcasted_iota(I32, (1, tq), 1)
    disti = qpos - kpos
    kposf = kpos.astype(F32)

    def branch(kv, allowed, m_r, a_r):
        pen = jnp.where(allowed, 0.0, NEG_INF)
        s = lax.dot_general(q4_ref[...], kv[:, :LANE], (((1,), (1,)), ((), ())), preferred_element_type=F32)
        s = jnp.concatenate([s[hh * tq:(hh + 1) * tq] + (pen + ALIBI[hh] * kposf) for hh in range(H_B)], axis=0)
        _flash_step(s, kv, m_r, a_r)

    expand = (lax.broadcasted_iota(I32, (n_blocks, tq), 0) ==
              (ki * tq + lax.broadcasted_iota(I32, (n_blocks, tq), 1)) // BLK).astype(BF16)
    chosen = jnp.dot(selm_ref[...], expand, preferred_element_type=F32) > 0.5
    branch(selb_ref[0], chosen & (disti >= 0), ms_ref, as_ref)

    @pl.when(ki >= qi - wt)
    def _():
        branch(winb_ref[0], (disti >= 0) & (disti < WINDOW), mw_ref, aw_ref)

    @pl.when(ki == qi)
    def _():
        a_s, a_w = as_ref[...], aw_ref[...]
        o_s = a_s[:, :LANE] / jnp.maximum(a_s[:, LANE:], 1e-30)
        o_w = a_w[:, :LANE] / jnp.maximum(a_w[:, LANE:], 1e-30)
        g = gt_ref[0]
        ys = _merge_heads([o_s[hh * tq:(hh + 1) * tq] for hh in range(H_B)])
        yw = _merge_heads([o_w[hh * tq:(hh + 1) * tq] for hh in range(H_B)])
        y = _head_cols(g, 0) * oc_ref[...] + _head_cols(g, 1) * ys + _head_cols(g, 2) * yw
        o_ref[0] = (y * _group_rsqrt(y, GROUP_W) * gout_ref[:, 256:512]).astype(BF16)


def _nsa_prompt(qn, gt, kvc, selb, winb, lw):
    nb, t, _ = qn.shape
    tq = min(512, t)
    nq = t // tq
    n_blocks = t // BLK
    wt = WINDOW // tq
    qi, ki = _causal_pairs(nq)

    def qmap(b, p, qi, ki):
        return (b, qi[p], 0)

    def kmap(b, p, qi, ki):
        return (b, ki[p], 0)

    def wmap(b, p, qi, ki):
        return (b, jnp.maximum(ki[p], qi[p] - wt), 0)

    grid_spec = pltpu.PrefetchScalarGridSpec(
        num_scalar_prefetch=2,
        grid=(nb, qi.shape[0]),
        in_specs=[pl.BlockSpec((1, tq, 256), qmap), pl.BlockSpec((1, tq, LANE), qmap),
                  pl.BlockSpec((1, n_blocks, LANE), lambda b, p, qi, ki: (b, 0, 0)),
                  pl.BlockSpec((1, tq, 256), kmap), pl.BlockSpec((1, tq, 256), wmap),
                  pl.BlockSpec((1, D_MODEL), lambda b, p, qi, ki: (0, 0))],
        out_specs=pl.BlockSpec((1, tq, 256), qmap),
        scratch_shapes=[pltpu.VMEM((H_B * tq, LANE), BF16), pltpu.VMEM((tq, n_blocks), BF16),
                        pltpu.VMEM((tq, 256), F32),
                        pltpu.VMEM((H_B * tq, LANE), F32), pltpu.VMEM((H_B * tq, 256), F32),
                        pltpu.VMEM((H_B * tq, LANE), F32), pltpu.VMEM((H_B * tq, 256), F32)])
    return pl.pallas_call(
        functools.partial(_nsa_kernel, tq, n_blocks, wt),
        out_shape=jax.ShapeDtypeStruct((nb, t, 256), BF16),
        grid_spec=grid_spec,
        compiler_params=_cparams(("arbitrary", "arbitrary")),
        name="nsa_prompt",
    )(qi, ki, qn, gt, kvc, selb, winb, lw['g_out'])


TOK_ROWS = D_MODEL // LANE


def _store_token_tiles(ref, x):
    n = x.shape[0]
    for j in range(TOK_ROWS):
        ref[pl.ds(j, n, stride=TOK_ROWS), :] = x[:, j * LANE:(j + 1) * LANE]


def _load_token_tiles(ref, n):
    return jnp.concatenate([ref[pl.ds(j, n, stride=TOK_ROWS), :] for j in range(TOK_ROWS)], axis=1)


def _outproj_kernel(ya_ref, yb_ref, yc_ref, yd_ref, x_ref, g1_ref, sc_ref, sh_ref, n2_ref, wo_ref, wr_ref, br_ref,
                    x1_ref, h2_ref, ti_ref, tw_ref):
    mix = jnp.zeros(x_ref.shape, F32)
    for g, y_ref in enumerate((ya_ref, yb_ref, yc_ref, yd_ref)):
        mix = mix + jnp.dot(y_ref[...].astype(BF16), wo_ref[g * GROUP_W:(g + 1) * GROUP_W, :],
                            preferred_element_type=F32)
    x1 = x_ref[...] + g1_ref[0] * mix
    x1_ref[...] = x1
    h2 = x1 * lax.rsqrt(jnp.mean(x1 * x1, -1, keepdims=True) + EPS) * n2_ref[...]
    h2 = h2 * (1.0 + sc_ref[0]) + sh_ref[0]
    _store_token_tiles(h2_ref, h2)
    logits = jnp.dot(h2, wr_ref[...], precision=HI, preferred_element_type=F32) + br_ref[...]
    lane = _lane_iota((1, LANE))
    ti = jnp.zeros(logits.shape, I32)
    tv = jnp.full(logits.shape, NEG_INF, F32)
    for k in range(TOP_K):
        m = jnp.max(logits, -1, keepdims=True)
        idx = jnp.min(jnp.where(logits == m, lane, LANE), -1, keepdims=True)
        ti = jnp.where(lane == k, idx, ti)
        tv = jnp.where(lane == k, m, tv)
        logits = jnp.where(lane == idx, -3e38, logits)
    e = jnp.where(lane < TOP_K, jnp.exp(tv - jnp.max(tv, -1, keepdims=True)), 0.0)
    ti_ref[...] = ti
    tw_ref[...] = e / jnp.sum(e, -1, keepdims=True)


def _outproj(ys, x, g1, sc2, sh2, lw, rows_per_mod):
    n, d = x.shape
    tm = min(512, n)
    per_row = rows_per_mod == 1
    if per_row:
        mod = pl.BlockSpec((1, tm, d), lambda i: (0, i, 0))
        g1, sc2, sh2 = [a.reshape(1, n, d) for a in (g1, sc2, sh2)]
    else:
        mod = pl.BlockSpec((1, 1, d), lambda i: (i * tm // rows_per_mod, 0, 0))
        g1, sc2, sh2 = [a.reshape(-1, 1, d) for a in (g1, sc2, sh2)]

    def row(w):
        return pl.BlockSpec((tm, w), lambda i: (i, 0))

    def const(shape):
        return pl.BlockSpec(shape, lambda i: (0, 0))

    return pl.pallas_call(
        _outproj_kernel,
        out_shape=[jax.ShapeDtypeStruct((n, d), F32), jax.ShapeDtypeStruct((n * TOK_ROWS, LANE), F32),
                   jax.ShapeDtypeStruct((n, LANE), I32), jax.ShapeDtypeStruct((n, LANE), F32)],
        grid=(n // tm,),
        in_specs=[row(256)] * 4 + [row(d), mod, mod, mod, const((1, d)), const((d, d)), const((d, LANE)),
                                    const((1, LANE))],
        out_specs=[row(d), pl.BlockSpec((tm * TOK_ROWS, LANE), lambda i: (i, 0)), row(LANE), row(LANE)],
        compiler_params=_cparams(("arbitrary",)),
        name="outproj",
    )(*ys, x, g1, sc2, sh2, lw['norm2'], lw['w_out'], lw['w_router'], lw['b_router'])


def _start_token_gather(idx_ref, n, src_hbm, buf, sem):
    def body(j, carry):
        r = idx_ref[0, 0, j]
        pltpu.make_async_copy(src_hbm.at[pl.ds(pl.multiple_of(r * TOK_ROWS, TOK_ROWS), TOK_ROWS)],
                              buf.at[pl.ds(pl.multiple_of(j * TOK_ROWS, TOK_ROWS), TOK_ROWS)], sem).start()
        return carry

    lax.fori_loop(0, n, body, 0, unroll=8)


def _wait_token_gather(n, src_hbm, buf, sem):
    pltpu.make_async_copy(src_hbm.at[pl.ds(0, n * TOK_ROWS)], buf, sem).wait()


MOE_TM = 512


def _moe_kernel(te_ref, tv_ref, idx0_ref, idxn_ref, h2_hbm, wgu_ref, bgu_ref, wd_ref, bd_ref, o_ref,
                wgu_bf, wd_bf, xbuf, sem):
    i = pl.program_id(0)
    n_tiles = pl.num_programs(0)
    prev = te_ref[jnp.maximum(i - 1, 0)]
    slot = i % 2

    @pl.when((i == 0) & (tv_ref[0] == 1))
    def _():
        _start_token_gather(idx0_ref, MOE_TM, h2_hbm, xbuf.at[0], sem.at[0])

    nxt = jnp.minimum(i + 1, n_tiles - 1)

    @pl.when((i + 1 < n_tiles) & (tv_ref[nxt] == 1))
    def _():
        _start_token_gather(idxn_ref, MOE_TM, h2_hbm, xbuf.at[1 - slot], sem.at[1 - slot])

    @pl.when((i == 0) | (te_ref[i] != prev))
    def _():
        for c in range(8):
            wgu_bf[c * 128:(c + 1) * 128, :] = wgu_ref[0, 0, c * 128:(c + 1) * 128, :].astype(BF16)
            wd_bf[c * 128:(c + 1) * 128, :] = wd_ref[0, 0, c * 128:(c + 1) * 128, :].astype(BF16)

    @pl.when(tv_ref[i] == 1)
    def _():
        _wait_token_gather(MOE_TM, h2_hbm, xbuf.at[slot], sem.at[slot])
        x = _load_token_tiles(xbuf.at[slot], MOE_TM).astype(BF16)
        hu = jnp.dot(x, wgu_bf[...], preferred_element_type=F32) + bgu_ref[0, 0]
        gt = jnp.minimum(hu[:, :D_FF], SWIGLU_LIMIT)
        up = jnp.clip(hu[:, D_FF:], -SWIGLU_LIMIT, SWIGLU_LIMIT)
        act = gt * jax.nn.sigmoid(SWIGLU_ALPHA * gt) * (up + 1.0)
        _store_token_tiles(o_ref, jnp.dot(act.astype(BF16), wd_bf[...], preferred_element_type=F32) + bd_ref[0, 0])

    @pl.when(tv_ref[i] == 0)
    def _():
        o_ref[...] = jnp.zeros_like(o_ref)


def _moe_ffn(h2_all, tok_of_row, tile_expert, tile_valid, layer, w_gu, b_gu, w_down, b_down):
    d = D_MODEL
    depth = w_gu.shape[0]
    n_tiles = tok_of_row.shape[0] // MOE_TM
    idx = tok_of_row.reshape(n_tiles, 1, MOE_TM)
    grid_spec = pltpu.PrefetchScalarGridSpec(
        num_scalar_prefetch=2,
        grid=(n_tiles,),
        in_specs=[pl.BlockSpec((1, 1, MOE_TM), lambda i, te, tv: (0, 0, 0), memory_space=pltpu.SMEM),
                  pl.BlockSpec((1, 1, MOE_TM), lambda i, te, tv: (jnp.minimum(i + 1, n_tiles - 1), 0, 0),
                               memory_space=pltpu.SMEM),
                  pl.BlockSpec(memory_space=pl.ANY),
                  pl.BlockSpec((1, 1, d, 2 * D_FF), lambda i, te, tv: (layer, te[i], 0, 0)),
                  pl.BlockSpec((1, 1, 1, 2 * D_FF), lambda i, te, tv: (layer, te[i], 0, 0)),
                  pl.BlockSpec((1, 1, D_FF, d), lambda i, te, tv: (layer, te[i], 0, 0)),
                  pl.BlockSpec((1, 1, 1, d), lambda i, te, tv: (layer, te[i], 0, 0))],
        out_specs=pl.BlockSpec((MOE_TM * TOK_ROWS, LANE), lambda i, te, tv: (i, 0)),
        scratch_shapes=[pltpu.VMEM((d, 2 * D_FF), BF16), pltpu.VMEM((D_FF, d), BF16),
                        pltpu.VMEM((2, MOE_TM * TOK_ROWS, LANE), F32), pltpu.SemaphoreType.DMA((2,))])
    return pl.pallas_call(
        _moe_kernel,
        out_shape=jax.ShapeDtypeStruct((n_tiles * MOE_TM * TOK_ROWS, LANE), F32),
        grid_spec=grid_spec,
        compiler_params=_cparams(("arbitrary",)),
        name="moe_ffn",
    )(tile_expert, tile_valid, idx, idx, h2_all, w_gu, b_gu.reshape(depth, N_EXP, 1, 2 * D_FF), w_down,
      b_down.reshape(depth, N_EXP, 1, d))


def _combine_kernel(tm, idx0_ref, idxn_ref, x1_ref, tw_ref, g2_ref, ys_hbm, o_ref, ybuf, sem):
    i = pl.program_id(0)
    n_tiles = pl.num_programs(0)
    slot = i % 2
    n_rows = TOP_K * tm

    @pl.when(i == 0)
    def _():
        _start_token_gather(idx0_ref, n_rows, ys_hbm, ybuf.at[0], sem.at[0])

    @pl.when(i + 1 < n_tiles)
    def _():
        _start_token_gather(idxn_ref, n_rows, ys_hbm, ybuf.at[1 - slot], sem.at[1 - slot])

    _wait_token_gather(n_rows, ys_hbm, ybuf.at[slot], sem.at[slot])
    tw = tw_ref[...]
    acc = jnp.zeros(x1_ref.shape, F32)
    for k in range(TOP_K):
        yk = _load_token_tiles(ybuf.at[slot, pl.ds(k * tm * TOK_ROWS, tm * TOK_ROWS)], tm)
        acc = acc + tw[:, k:k + 1] * yk
    o_ref[...] = x1_ref[...] + g2_ref[0] * acc


COMBINE_TM = 256


def _combine(x1, ys, rows, tw, g2, rows_per_mod):
    n, d = x1.shape
    tm = min(COMBINE_TM, n)
    n_tiles = n // tm
    idx = jnp.transpose(rows.reshape(n_tiles, tm, TOP_K), (0, 2, 1)).reshape(n_tiles, 1, TOP_K * tm)
    per_row = rows_per_mod == 1
    if per_row:
        mod = pl.BlockSpec((1, tm, d), lambda i: (0, i, 0))
        g2 = g2.reshape(1, n, d)
    else:
        mod = pl.BlockSpec((1, 1, d), lambda i: (i * tm // rows_per_mod, 0, 0))
        g2 = g2.reshape(-1, 1, d)
    return pl.pallas_call(
        functools.partial(_combine_kernel, tm),
        out_shape=jax.ShapeDtypeStruct((n, d), F32),
        grid=(n_tiles,),
        in_specs=[pl.BlockSpec((1, 1, TOP_K * tm), lambda i: (0, 0, 0), memory_space=pltpu.SMEM),
                  pl.BlockSpec((1, 1, TOP_K * tm), lambda i: (jnp.minimum(i + 1, n_tiles - 1), 0, 0),
                               memory_space=pltpu.SMEM),
                  pl.BlockSpec((tm, d), lambda i: (i, 0)), pl.BlockSpec((tm, LANE), lambda i: (i, 0)), mod,
                  pl.BlockSpec(memory_space=pl.ANY)],
        out_specs=pl.BlockSpec((tm, d), lambda i: (i, 0)),
        scratch_shapes=[pltpu.VMEM((2, TOP_K * tm * TOK_ROWS, LANE), F32), pltpu.SemaphoreType.DMA((2,))],
        compiler_params=_cparams(("arbitrary",)),
        name="combine",
    )(idx, idx, x1, tw, g2, ys)


def _moe(h2_all, ti_all, lw_moe):
    n = h2_all.shape[0] // TOK_ROWS
    pairs = n * TOP_K
    e = ti_all[:, :TOP_K].reshape(pairs)
    onehot = (e[:, None] == jnp.arange(N_EXP, dtype=I32)[None, :]).astype(I32)
    csum = jnp.cumsum(onehot, axis=0)
    rank = jnp.take_along_axis(csum, e[:, None], axis=1)[:, 0] - 1
    counts = csum[-1]
    padded = (counts + MOE_TM - 1) // MOE_TM * MOE_TM
    ends = jnp.cumsum(padded)
    starts = ends - padded
    row = starts[e] + rank
    r_max = -(-pairs // MOE_TM) * MOE_TM + N_EXP * MOE_TM
    tok_of_row = jnp.zeros((r_max,), I32).at[row].set(jnp.arange(pairs, dtype=I32) // TOP_K)
    tile_start = jnp.arange(r_max // MOE_TM, dtype=I32) * MOE_TM
    tile_valid = (tile_start < ends[-1]).astype(I32)
    tile_expert = jnp.minimum(jnp.sum((tile_start[:, None] >= ends[None, :]).astype(I32), axis=1), N_EXP - 1)
    tile_expert = jnp.where(tile_valid == 1, tile_expert, jnp.max(jnp.where(counts > 0, jnp.arange(N_EXP), 0)))
    ys = _moe_ffn(h2_all, tok_of_row, tile_expert, tile_valid, lw_moe['layer'], lw_moe['w_gu'], lw_moe['b_gu'],
                  lw_moe['w_down'], lw_moe['b_down'])
    return ys, row.reshape(n, TOP_K)


DEC_PAGES = 16


def _rows_to_headmajor(o8, width):
    wide = jnp.concatenate([o8] * H_B, axis=1)
    row = lax.broadcasted_iota(I32, wide.shape, 0)
    lane = lax.broadcasted_iota(I32, wide.shape, 1)
    return jnp.sum(jnp.where(row == lane // width, wide, 0.0), axis=0, keepdims=True)


def _head_rows(q_row, width, pad_to):
    rows = [q_row[:, hh * width:(hh + 1) * width] for hh in range(H_B)]
    q4 = jnp.concatenate(rows + [jnp.zeros((8 - H_B, width), F32)], axis=0)
    if pad_to > width:
        q4 = jnp.concatenate([q4, jnp.zeros((8, pad_to - width), F32)], axis=1)
    return q4


def _slope_rows():
    row = lax.broadcasted_iota(I32, (8, 1), 0)
    out = jnp.zeros((8, 1), F32)
    for hh in range(H_B):
        out = jnp.where(row == hh, ALIBI[hh], out)
    return out


def _nt(a, b, **kw):
    return lax.dot_general(a, b, (((1,), (1,)), ((), ())), preferred_element_type=F32, **kw)


def _decode_stream_kernel(layer, n_pages, pg, pt_ref, qm_ref, kh_ref, cv_ref, gk_ref, wc_ref, wuvw_ref,
                          gout_ref, c_hbm, krt_hbm, cmp_hbm, ya_ref, ksum_ref,
                          cbuf, krbuf, cmpbuf, sem):
    b = pl.program_id(0)
    n_samples = pl.num_programs(0)
    n_chunks = n_pages // pg

    def copies(bb, chunk, slot):
        out = []
        for j in range(pg):
            page = pt_ref[bb, chunk * pg + j]
            rows = pl.ds(j * LANE, LANE)
            out.append(pltpu.make_async_copy(c_hbm.at[layer, page], cbuf.at[slot, rows], sem.at[0, slot]))
            out.append(pltpu.make_async_copy(krt_hbm.at[layer, page], krbuf.at[slot, j], sem.at[1, slot]))
            out.append(pltpu.make_async_copy(cmp_hbm.at[layer, page], cmpbuf.at[slot, rows], sem.at[2, slot]))
        return out

    @pl.when(b == 0)
    def _():
        for cp in copies(0, 0, 0):
            cp.start()

    q32 = qm_ref[0].astype(F32)
    qlat = jnp.concatenate([q32[:, hh * 256:hh * 256 + D_LAT] for hh in range(H_A)] +
                           [jnp.zeros((8 - H_A, D_LAT), F32)], axis=0)
    qrot = jnp.concatenate([q32[:, hh * 256 + D_LAT + hh * D_ROPE:hh * 256 + D_LAT + (hh + 1) * D_ROPE]
                            for hh in range(H_A)] + [jnp.zeros((8 - H_A, D_ROPE), F32)], axis=0)
    kh_new = kh_ref[0].astype(F32)
    m0 = (jnp.sum(qlat * kh_new[:, :D_LAT], -1, keepdims=True) +
          jnp.sum(qrot * kh_new[:, D_LAT:D_LAT + D_ROPE], -1, keepdims=True))
    l0 = jnp.ones((8, 1), F32)
    acc0 = jnp.broadcast_to(cv_ref[0][:, :D_LAT].astype(F32), (8, D_LAT))
    qlat_b, qrot_b = qlat.astype(BF16), qrot.astype(BF16)
    ones_c = jnp.ones((8, D_LAT), BF16)
    gk = gk_ref[...]
    wc = wc_ref[...]

    def body(chunk, carry):
        m_prev, l_prev, acc = carry
        slot = (b * n_chunks + chunk) % 2
        last = chunk + 1 == n_chunks
        nxt_b = jnp.where(last, b + 1, b)
        nxt_c = jnp.where(last, 0, chunk + 1)

        @pl.when(nxt_b < n_samples)
        def _():
            for cp in copies(nxt_b, nxt_c, 1 - slot):
                cp.start()

        for cp in copies(b, chunk, slot):
            cp.wait()
        c = cbuf[slot]
        krt = krbuf[slot]
        s_rot = jnp.concatenate([jnp.dot(qrot_b, krt[j].astype(BF16), preferred_element_type=F32)
                                 for j in range(pg)], axis=1)
        ss_rot = jnp.concatenate([jnp.sum(krt[j] * krt[j], axis=0, keepdims=True) for j in range(pg)], axis=1)
        ss = _nt(ones_c, (c * c).astype(BF16)) + ss_rot
        r = lax.rsqrt(ss * (1.0 / MLA_DIM) + EPS)
        s = (_nt(qlat_b, (c * gk).astype(BF16)) + s_rot) * r
        m_new = jnp.maximum(m_prev, jnp.max(s, -1, keepdims=True))
        p = jnp.exp(s - m_new)
        alpha = jnp.exp(m_prev - m_new)
        l_new = alpha * l_prev + jnp.sum(p, -1, keepdims=True)
        acc = alpha * acc + jnp.dot(p.astype(BF16), c.astype(BF16), preferred_element_type=F32)
        cm = cmpbuf[slot]
        ksum = jnp.sum(cm.reshape(2 * pg, BLK, LANE) * wc[None], axis=1)
        ksum_ref[0, pl.ds(pl.multiple_of(chunk * 2 * pg, 2 * pg), 2 * pg), :] = ksum
        return m_new, l_new, acc

    m_f, l_f, acc = lax.fori_loop(0, n_chunks, body, (m0, l0, acc0))
    o = acc / l_f
    yw = jnp.dot(o.astype(BF16), wuvw_ref[...], preferred_element_type=F32)
    row = lax.broadcasted_iota(I32, yw.shape, 0)
    lane2 = lax.broadcasted_iota(I32, yw.shape, 1)
    ya = jnp.sum(jnp.where(row == lane2 // D_VA, yw, 0.0), axis=0, keepdims=True)
    ya_ref[0] = ya * _group_rsqrt(ya, GROUP_W) * gout_ref[:, 0:256]


def _decode_stream(layer, page_table, qm, kh, cv, lw, cache_c, cache_krt, cache_cmp):
    db, n_pages = page_table.shape
    n_blk = 2 * n_pages
    pg = min(DEC_PAGES, n_pages)

    def per(w):
        return pl.BlockSpec((1, 1, w), lambda b, pt: (b, 0, 0))

    def const(shape):
        return pl.BlockSpec(shape, lambda b, pt: (0,) * len(shape))

    anyspec = pl.BlockSpec(memory_space=pl.ANY)
    grid_spec = pltpu.PrefetchScalarGridSpec(
        num_scalar_prefetch=1,
        grid=(db,),
        in_specs=[per(1024), per(256), per(256), const((1, LANE)), const((BLK, LANE)),
                  const((LANE, 256)), const((1, D_MODEL)), anyspec, anyspec, anyspec],
        out_specs=[per(256), pl.BlockSpec((1, n_blk, LANE), lambda b, pt: (b, 0, 0))],
        scratch_shapes=[pltpu.VMEM((2, pg * LANE, LANE), F32), pltpu.VMEM((2, pg, D_ROPE, LANE), F32),
                        pltpu.VMEM((2, pg * LANE, LANE), F32), pltpu.SemaphoreType.DMA((3, 2))])
    ya, ksum = pl.pallas_call(
        functools.partial(_decode_stream_kernel, layer, n_pages, pg),
        out_shape=[jax.ShapeDtypeStruct((db, 1, 256), F32), jax.ShapeDtypeStruct((db, n_blk, LANE), F32)],
        grid_spec=grid_spec,
        compiler_params=_cparams(("arbitrary",)),
        name="decode_stream",
    )(page_table, qm.reshape(db, 1, 1024), kh.reshape(db, 1, 256), cv.reshape(db, 1, 256),
      lw['g_k_mla'], lw['wc'], lw['wuv_wide'], lw['g_out'], cache_c, cache_krt, cache_cmp)
    return ya.reshape(db, 256), ksum


CMP_GROUP = 8


def _cmp_select_kernel(n_blk, past, ksum_ref, qn_ref, gcmp_ref, oc_ref, sel_ref):
    lane = _lane_iota((1, LANE))
    khalf = lane < D_NSA
    jb = lax.broadcasted_iota(I32, (1, n_blk), 1)
    cpos = (jb + 1) * BLK - 1
    bias = _slope_rows() * (past - cpos).astype(F32)
    ocs, imps = [], []
    for i in range(CMP_GROUP):
        ksum = ksum_ref[i]
        ssq = jnp.sum(jnp.where(khalf, ksum * ksum, 0.0), -1, keepdims=True)
        kvc = jnp.where(khalf, ksum * lax.rsqrt(ssq * (1.0 / D_NSA) + EPS) * gcmp_ref[...], ksum)
        q8 = _head_rows(qn_ref[i:i + 1, :], D_NSA, D_NSA)
        s_c = _nt(q8, kvc[:, :D_NSA], precision=HI) - bias
        e_c = jnp.exp(s_c - jnp.max(s_c, -1, keepdims=True))
        p_c = e_c / jnp.maximum(jnp.sum(e_c, -1, keepdims=True), 1e-30)
        oc8 = jnp.dot(p_c.astype(BF16), kvc[:, D_NSA:].astype(BF16), preferred_element_type=F32)
        ocs.append(_rows_to_headmajor(oc8, D_NSA))
        hrow = lax.broadcasted_iota(I32, p_c.shape, 0)
        imps.append(jnp.sum(jnp.where(hrow < H_B, p_c, 0.0), axis=0, keepdims=True))
    oc_ref[...] = jnp.concatenate(ocs, axis=0)
    imp = jnp.where(jb == 0, FORCE_SCORE, jnp.concatenate(imps, axis=0))
    sel = jnp.zeros((CMP_GROUP, LANE), I32)
    for k in range(N_SEL - 1):
        mx = jnp.max(imp, -1, keepdims=True)
        idx = jnp.min(jnp.where(imp == mx, jb, n_blk), -1, keepdims=True)
        sel = jnp.where(lane == k, idx, sel)
        imp = jnp.where(jb == idx, -3e38, imp)
    sel_ref[...] = sel


def _cmp_select(ksum, qn, lw, past):
    db, n_blk, _ = ksum.shape
    return pl.pallas_call(
        functools.partial(_cmp_select_kernel, n_blk, past),
        out_shape=[jax.ShapeDtypeStruct((db, 256), F32), jax.ShapeDtypeStruct((db, LANE), I32)],
        grid=(db // CMP_GROUP,),
        in_specs=[pl.BlockSpec((CMP_GROUP, n_blk, LANE), lambda g: (g, 0, 0)),
                  pl.BlockSpec((CMP_GROUP, 256), lambda g: (g, 0)),
                  pl.BlockSpec((1, LANE), lambda g: (0, 0))],
        out_specs=[pl.BlockSpec((CMP_GROUP, 256), lambda g: (g, 0)),
                   pl.BlockSpec((CMP_GROUP, LANE), lambda g: (g, 0))],
        compiler_params=_cparams(("arbitrary",)),
        name="cmp_select",
    )(ksum, qn, lw['g_k_cmp'])


def _decode_sel_kernel(layer, n_pages, w_buf, pt_ref, si_ref, qn_ref, gt_ref, oc_ref, snew_ref, wnew_ref, win_ref,
                       gout_ref, sel_hbm, yb_ref, selbuf, sem):
    b = pl.program_id(0)
    n_sel = N_SEL - 1
    past = n_pages * LANE

    def copies():
        out = []
        for r in range(n_sel):
            j = si_ref[b, r]
            page = pt_ref[b, j // 2]
            off = pl.multiple_of((j % 2) * BLK, BLK)
            out.append(pltpu.make_async_copy(sel_hbm.at[layer, page, pl.ds(off, BLK)],
                                             selbuf.at[pl.ds(r * BLK, BLK)], sem))
        return out

    for cp in copies():
        cp.start()
    q8 = _head_rows(qn_ref[0], D_NSA, LANE)
    q8b = q8.astype(BF16)
    q8r = q8b.astype(F32)
    slope = _slope_rows()
    lane = _lane_iota((1, LANE))

    def branch(kv, dist, valid, new_row):
        kvb = kv.astype(BF16)
        s = _nt(q8b, kvb) - slope * dist
        if valid is not None:
            s = jnp.where(valid, s, NEG_INF)
        newb = new_row.astype(BF16).astype(F32)
        s_new = jnp.sum(jnp.where(lane < D_NSA, q8r * newb, 0.0), -1, keepdims=True)
        m = jnp.maximum(jnp.max(s, -1, keepdims=True), s_new)
        e = jnp.exp(s - m)
        if valid is not None:
            e = jnp.where(valid, e, 0.0)
        e_new = jnp.exp(s_new - m)
        den = jnp.sum(e, -1, keepdims=True) + e_new
        o = (jnp.dot(e.astype(BF16), kvb, preferred_element_type=F32) + e_new * newb) / den
        return _rows_to_headmajor(pltpu.roll(o, D_NSA, 1)[:, :D_NSA], D_NSA)

    iw = lax.broadcasted_iota(I32, (1, w_buf), 1)
    distw = w_buf - iw
    validw = (distw < WINDOW) & (past - w_buf + iw >= 0)
    yw = branch(win_ref[0, 0], distw.astype(F32), validw, wnew_ref[0])

    for cp in copies():
        cp.wait()
    ls = lax.broadcasted_iota(I32, (1, n_sel * BLK), 1)
    spos = ls % BLK
    for r in range(n_sel):
        spos = spos + jnp.where(ls // BLK == r, si_ref[b, r] * BLK, 0)
    ys = branch(selbuf[...], (past - spos).astype(F32), None, snew_ref[0])
    g = gt_ref[0]
    y = _head_cols(g, 0) * oc_ref[0] + _head_cols(g, 1) * ys + _head_cols(g, 2) * yw
    yb_ref[0] = y * _group_rsqrt(y, GROUP_W) * gout_ref[:, 256:512]


def _decode_sel(layer, page_table, sel_idx, qn, gt, oc, sel_new, win_new, win_cache, lw, cache_sel):
    db, n_pages = page_table.shape
    w_buf = win_cache.shape[2]

    def per(w):
        return pl.BlockSpec((1, 1, w), lambda b, pt, si: (b, 0, 0))

    grid_spec = pltpu.PrefetchScalarGridSpec(
        num_scalar_prefetch=2,
        grid=(db,),
        in_specs=[per(256), per(LANE), per(256), per(LANE), per(LANE),
                  pl.BlockSpec((1, 1, w_buf, LANE), lambda b, pt, si: (layer, b, 0, 0)),
                  pl.BlockSpec((1, D_MODEL), lambda b, pt, si: (0, 0)),
                  pl.BlockSpec(memory_space=pl.ANY)],
        out_specs=per(256),
        scratch_shapes=[pltpu.VMEM(((N_SEL - 1) * BLK, LANE), F32), pltpu.SemaphoreType.DMA])
    yb = pl.pallas_call(
        functools.partial(_decode_sel_kernel, layer, n_pages, w_buf),
        out_shape=jax.ShapeDtypeStruct((db, 1, 256), F32),
        grid_spec=grid_spec,
        compiler_params=_cparams(("arbitrary",)),
        name="decode_sel",
    )(page_table, sel_idx, qn.reshape(db, 1, 256), gt.reshape(db, 1, LANE), oc.reshape(db, 1, 256),
      sel_new.reshape(db, 1, LANE),
      win_new.reshape(db, 1, LANE), win_cache, lw['g_out'], cache_sel)
    return yb.reshape(db, 256)


def kernel(x_prompt, x_sample, cache_mla_c, cache_mla_kr, cache_nsa_cmp_kv, cache_nsa_sel_kv, cache_nsa_win_kv,
           state_pool, page_table, c_prompt, c_sample, w_ada, b_ada, norm1, norm2, w_in, w_uk, w_uv, g_kv, g_q_mla,
           g_k_mla, g_q_nsa, g_k_cmp, g_k_sel, g_k_win, w_cmp_k, w_cmp_v, w_pool, pool_scale, g_sgu, w_s, b_s,
           g_out, w_out, w_router, b_router, w_gu, b_gu, w_down, b_down):
    w = dict(norm1=norm1, norm2=norm2, w_in=w_in, w_uk=w_uk, w_uv=w_uv, g_kv=g_kv, g_q_mla=g_q_mla, g_k_mla=g_k_mla,
             g_q_nsa=g_q_nsa, g_k_cmp=g_k_cmp, g_k_sel=g_k_sel, g_k_win=g_k_win, w_cmp_k=w_cmp_k, w_cmp_v=w_cmp_v,
             w_pool=w_pool, pool_scale=pool_scale, g_sgu=g_sgu, w_s=w_s, b_s=b_s, g_out=g_out, w_out=w_out,
             w_router=w_router, b_router=b_router, w_gu=w_gu, b_gu=b_gu, w_down=w_down, b_down=b_down)
    nb, t, d = x_prompt.shape
    db = x_sample.shape[0]
    depth = w_ada.shape[0]
    n_tok = nb * t
    past = page_table.shape[1] * cache_mla_c.shape[2]
    w_keep = min(WINDOW, t)

    rows = -(-(nb + db) // 8) * 8
    c_all = jnp.concatenate([c_prompt, c_sample, jnp.zeros((rows - nb - db, d), F32)], axis=0)
    ada = _ada(c_all, w_ada, b_ada)
    cache_krt = jnp.swapaxes(cache_mla_kr, 2, 3)

    xp = x_prompt
    xs = x_sample.reshape(1, db, d)
    st = [[] for _ in range(13)]
    for l in range(depth):
        lw = _layer_weights(l, w)
        sh1p, sc1p, g1p, sh2p, sc2p, g2p = [z[:, None, :] for z in jnp.split(ada[l, :nb], 6, axis=-1)]
        sh1s, sc1s, g1s, sh2s, sc2s, g2s = jnp.split(ada[l, nb:nb + db], 6, axis=-1)

        o = _inproj(xp, sh1p, sc1p, lw, 0)
        ya = _mla_prompt(o['qm'], o['kh'], o['cv'], lw)
        yb = _nsa_prompt(o['qn'], o['gt'], o['kvc'], o['selb'], o['winb'], lw)
        ys = [y.reshape(n_tok, GROUP_W) for y in (ya, yb, o['yc'], o['yd'])]
        x1p, h2p, tip, twp = _outproj(ys, xp.reshape(n_tok, d), g1p, sc2p, sh2p, lw, t)

        s = _inproj(xs, sh1s[None], sc1s[None], lw, past, pool_buf=state_pool[l])
        ya_s, ksum_s = _decode_stream(l, page_table, s['qm'][0], s['kh'][0], s['cv'][0], lw,
                                      cache_mla_c, cache_krt, cache_nsa_cmp_kv)
        oc_s, sel_s = _cmp_select(ksum_s, s['qn'][0], lw, past)
        yb_s = _decode_sel(l, page_table, sel_s, s['qn'][0], s['gt'][0], oc_s, s['sel'][0],
                           s['win'][0], cache_nsa_win_kv, lw, cache_nsa_sel_kv)
        x1s, h2s, tis, tws = _outproj([ya_s, yb_s, s['yc'][0], s['yd'][0]], xs[0], g1s, sc2s, sh2s, lw, 1)

        ys_moe, rows = _moe(jnp.concatenate([h2p, h2s], axis=0), jnp.concatenate([tip, tis], axis=0), lw['moe'])
        xp = _combine(x1p, ys_moe, rows[:n_tok], twp, g2p, t).reshape(nb, t, d)
        xs = _combine(x1s, ys_moe, rows[n_tok:], tws, g2s, 1).reshape(1, db, d)

        per_l = (o['cl'], s['cl'][0][:, None, :], o['kr'], s['kr'][0][:, None, :], o['cmp'], s['cmp'][0][:, None, :],
                 o['sel'], s['sel'][0][:, None, :], o['win'][:, t - w_keep:],
                 jnp.concatenate([cache_nsa_win_kv[l][:, 1:], s['win'][0][:, None, :]], axis=1),
                 o['zc_tail'][:, 16 - POOL_PAD:],
                 jnp.concatenate([state_pool[l][:, 1:], s['zc_tail'][0][:, None, :]], axis=1),
                 s['v_tail'][0][:, None, :])
        for k, v in enumerate(per_l):
            st[k].append(v)
    return (xp, xs.reshape(db, 1, d)) + tuple(jnp.stack(v, axis=0) for v in st)
```

```python
import functools

import numpy as np
import jax
import jax.numpy as jnp
from jax import lax
from jax.experimental import pallas as pl
from jax.experimental.pallas import tpu as pltpu

F32 = jnp.float32
BF16 = jnp.bfloat16
I32 = jnp.int32
HI = lax.Precision.HIGHEST

EPS = 1e-6
NEG_INF = -1e30
D_MODEL = 1024
GROUP_W = 256
H_A, D_NOPE, D_ROPE, D_LAT, D_VA = 4, 64, 32, 128, 64
MLA_DIM = D_LAT + D_ROPE
MLA_SCALE = (D_NOPE + D_ROPE) ** -0.5
ROPE_BASE = 10000.0
H_B, D_NSA, BLK, N_SEL, WINDOW = 4, 64, 64, 16, 512
NSA_SCALE = D_NSA ** -0.5
FORCE_SCORE = 1e4
POOL_WINDOWS = (2, 4, 8, 16)
POOL_PAD = 15
H_D, CHUNK, D_VD = 4, 128, 64
N_EXP, TOP_K, D_FF = 32, 4, 1024
SWIGLU_LIMIT, SWIGLU_ALPHA = 7.0, 1.702
ALIBI = tuple(2.0 ** (-8.0 * (i + 1) / H_B) for i in range(H_B))

LANE = 128
VMEM_LIMIT = 56 * 1024 * 1024

C_QN, C_QR, C_QRS, C_CA, C_KR, C_KRS, C_QB, C_CMP, C_SEL, C_WIN, C_ZG, C_ZC, C_ZD, N_PAD = (
    0, 256, 384, 512, 640, 768, 896, 1152, 1280, 1408, 1536, 1664, 1920, 2432)


def _in_cols():
    qa, ca, ra, qb = 0, 384, 512, 544
    cmp_, sel, win, zg, zc, zd, n_in = 800, 928, 1056, 1184, 1196, 1452, 1964
    per = D_NOPE + D_ROPE
    half = D_ROPE // 2
    cols = []
    for h in range(H_A):
        cols += list(range(qa + h * per, qa + h * per + D_NOPE))
    for h in range(H_A):
        cols += list(range(qa + h * per + D_NOPE, qa + (h + 1) * per))
    for h in range(H_A):
        b = qa + h * per + D_NOPE
        cols += list(range(b + half, b + D_ROPE)) + list(range(b, b + half))
    cols += list(range(ca, ca + D_LAT))
    cols += list(range(ra, ra + D_ROPE)) * 4
    cols += (list(range(ra + half, ra + D_ROPE)) + list(range(ra, ra + half))) * 4
    cols += list(range(qb, qb + 256))
    cols += list(range(cmp_, cmp_ + 128)) + list(range(sel, sel + 128)) + list(range(win, win + 128))
    cols += list(range(zg, zg + 12)) + [n_in] * (LANE - 12)
    cols += list(range(zc, zc + 256))
    cols += list(range(zd, zd + 512))
    assert len(cols) == N_PAD
    return np.asarray(cols, np.int32)


_IN_COLS = _in_cols()


def _cparams(sem):
    return pltpu.CompilerParams(dimension_semantics=sem, vmem_limit_bytes=VMEM_LIMIT)


def _lane_iota(shape):
    return lax.broadcasted_iota(I32, shape, len(shape) - 1)


def _group_rsqrt(x, width):
    n = x.shape[-1]
    sq = x * x
    if width == n:
        return lax.rsqrt(jnp.sum(sq, -1, keepdims=True) * (1.0 / width) + EPS)
    lane = _lane_iota((1, n))
    out = jnp.zeros_like(x)
    for g in range(n // width):
        m = (lane >= g * width) & (lane < (g + 1) * width)
        ss = jnp.sum(jnp.where(m, sq, 0.0), -1, keepdims=True)
        out = jnp.where(m, lax.rsqrt(ss * (1.0 / width) + EPS), out)
    return out


def _ada_kernel(c_ref, w_ref, b_ref, o_ref):
    c = c_ref[...]
    s = c * jax.nn.sigmoid(c)
    o_ref[0] = jnp.dot(s.astype(BF16), w_ref[0].astype(BF16), preferred_element_type=F32) + b_ref[0]


def _ada(c_all, w_ada, b_ada):
    depth, d, n = w_ada.shape
    rows = c_all.shape[0]
    tn = 1536
    return pl.pallas_call(
        _ada_kernel,
        out_shape=jax.ShapeDtypeStruct((depth, rows, n), F32),
        grid=(depth, n // tn),
        in_specs=[pl.BlockSpec((rows, d), lambda l, j: (0, 0)),
                  pl.BlockSpec((1, d, tn), lambda l, j: (l, 0, j)),
                  pl.BlockSpec((1, 1, tn), lambda l, j: (l, 0, j))],
        out_specs=pl.BlockSpec((1, rows, tn), lambda l, j: (l, 0, j)),
        compiler_params=_cparams(("arbitrary", "arbitrary")),
        name="ada",
    )(c_all, w_ada, b_ada.reshape(depth, 1, n))


def _gelu_tanh(x):
    return 0.5 * x * (1.0 + jnp.tanh(0.7978845608028654 * (x + 0.044715 * (x * x * x))))


def _inproj_kernel(sample, tm, x_ref, sh_ref, sc_ref, n1_ref, w_ref, cos_ref, sin_ref, wuk_ref, gq_ref,
                   gkv_ref, gk_ref, gqn_ref, gcmp_ref, gsel_ref, gwin_ref, wc_ref, wpool_ref, pscale_ref,
                   gsgu_ref, ws_ref, bs_ref, gout_ref, pool_ref,
                   qm_ref, kh_ref, cl_ref, cv_ref, kr_ref, qn_ref, gt_ref, cmp_ref, sel_ref, selb_ref,
                   win_ref, winb_ref, kvc_ref, zc_ref, v_ref, yc_ref, yd_ref, carry_ref):
    i = pl.program_id(1)
    x = x_ref[0]
    xn = x * lax.rsqrt(jnp.mean(x * x, -1, keepdims=True) + EPS) * n1_ref[...]
    h = xn * (1.0 + sc_ref[0]) + sh_ref[0]
    z = jnp.dot(h.astype(BF16), w_ref[...], preferred_element_type=F32)
    cos4, sin4 = cos_ref[...], sin_ref[...]
    lane = _lane_iota((1, LANE))

    qlat = jnp.dot(z[:, C_QN:C_QN + 256].astype(BF16), wuk_ref[...], preferred_element_type=F32)
    qrot = z[:, C_QR:C_QR + LANE] * cos4 + z[:, C_QRS:C_QRS + LANE] * sin4
    for hh in range(H_A):
        lat = qlat[:, hh * D_LAT:(hh + 1) * D_LAT]
        rot = jnp.where((lane >= hh * D_ROPE) & (lane < (hh + 1) * D_ROPE), qrot, 0.0)
        ss = jnp.sum(lat * lat, -1, keepdims=True) + jnp.sum(rot * rot, -1, keepdims=True)
        r = lax.rsqrt(ss * (1.0 / MLA_DIM) + EPS) * MLA_SCALE
        qm_ref[0, :, hh * 256:hh * 256 + D_LAT] = (lat * r * gq_ref[:, hh * D_LAT:(hh + 1) * D_LAT]).astype(BF16)
        qm_ref[0, :, hh * 256 + D_LAT:(hh + 1) * 256] = (rot * r).astype(BF16)

    ca = z[:, C_CA:C_CA + LANE]
    c_lat = ca * _group_rsqrt(ca, LANE) * gkv_ref[...]
    kr4 = z[:, C_KR:C_KR + LANE] * cos4 + z[:, C_KRS:C_KRS + LANE] * sin4
    cl_ref[0] = c_lat
    ones_blk = jnp.ones((tm, LANE), BF16)
    cv_ref[0, :, :LANE] = c_lat.astype(BF16)
    cv_ref[0, :, LANE:] = ones_blk
    kr_ref[0] = kr4[:, :D_ROPE]
    ssk = jnp.sum(c_lat * c_lat, -1, keepdims=True) + 0.25 * jnp.sum(kr4 * kr4, -1, keepdims=True)
    rk = lax.rsqrt(ssk * (1.0 / MLA_DIM) + EPS)
    kh_ref[0, :, :LANE] = (c_lat * rk * gk_ref[...]).astype(BF16)
    kh_ref[0, :, LANE:] = (kr4 * rk).astype(BF16)

    zqb = z[:, C_QB:C_QB + 256]
    qn_ref[0] = zqb * _group_rsqrt(zqb, D_NSA) * gqn_ref[...] * NSA_SCALE
    gt_ref[0] = jax.nn.sigmoid(z[:, C_ZG:C_ZG + LANE])
    cmp_rows = z[:, C_CMP:C_CMP + LANE]
    cmp_ref[0] = cmp_rows
    khalf = lane < D_NSA

    def norm_k(zz, g_ref):
        ss = jnp.sum(jnp.where(khalf, zz * zz, 0.0), -1, keepdims=True)
        return jnp.where(khalf, zz * lax.rsqrt(ss * (1.0 / D_NSA) + EPS) * g_ref[...], zz)

    sel_rows = norm_k(z[:, C_SEL:C_SEL + LANE], gsel_ref)
    win_rows = norm_k(z[:, C_WIN:C_WIN + LANE], gwin_ref)
    sel_ref[0] = sel_rows
    selb_ref[0, :, :LANE] = sel_rows.astype(BF16)
    selb_ref[0, :, LANE:] = ones_blk
    win_ref[0] = win_rows
    winb_ref[0, :, :LANE] = win_rows.astype(BF16)
    winb_ref[0, :, LANE:] = ones_blk

    zc = z[:, C_ZC:C_ZC + 256]
    zd = z[:, C_ZD:C_ZD + 512]
    uv = _gelu_tanh(zd)
    u = uv[:, :256]
    v = uv[:, 256:]
    v = v * _group_rsqrt(v, D_VD) * gsgu_ref[...]
    lane2 = _lane_iota((1, 256))
    wsel = [lane2 < 64, lane2 < 128, lane2 < 192]

    def pick(a2, a4, a8, a16):
        return jnp.where(wsel[0], a2, jnp.where(wsel[1], a4, jnp.where(wsel[2], a8, a16)))

    wlane = pick(2.0, 4.0, 8.0, 16.0)
    if not sample:
        nbk = tm // BLK
        ksum = jnp.sum(cmp_rows.reshape(nbk, BLK, LANE) * wc_ref[...][None], axis=1)
        kvc_ref[0] = norm_k(ksum, gcmp_ref)

        @pl.when(i == 0)
        def _():
            carry_ref[...] = jnp.zeros_like(carry_ref)

        ext = jnp.concatenate([carry_ref[...], zc], axis=0)
        a2 = ext + pltpu.roll(ext, 1, 0)
        a4 = a2 + pltpu.roll(a2, 2, 0)
        a8 = a4 + pltpu.roll(a4, 4, 0)
        a16 = a8 + pltpu.roll(a8, 8, 0)
        wsum = pick(a2, a4, a8, a16)[16:]
        carry_ref[...] = zc[tm - 16:]
        pos = (i * tm + lax.broadcasted_iota(I32, (tm, 1), 0)).astype(F32)
        cnt = jnp.minimum(pos + 1.0, wlane)
        zc_ref[0] = zc[tm - 16:]
        v_ref[0] = v[tm - 16:]
        for c in range(tm // CHUNK):
            vc = v[c * CHUNK:(c + 1) * CHUNK]
            vst = jnp.concatenate(
                [jnp.where((lane2 >= hh * D_VD) & (lane2 < (hh + 1) * D_VD), vc, 0.0) for hh in range(H_D)], axis=0)
            mixed = jnp.dot(ws_ref[...], vst.astype(BF16), preferred_element_type=F32) + bs_ref[...]
            yd = u[c * CHUNK:(c + 1) * CHUNK] * mixed
            yd_ref[0, c * CHUNK:(c + 1) * CHUNK, :] = (
                yd * _group_rsqrt(yd, GROUP_W) * gout_ref[:, 768:1024]).astype(BF16)
    else:
        kvc_ref[0] = jnp.zeros_like(kvc_ref[0])
        pb = pool_ref[...]
        sums = []
        for w in POOL_WINDOWS:
            sums.append(zc + jnp.sum(pb[:, POOL_PAD - (w - 1):, :], axis=1))
        wsum = pick(*sums)
        cnt = wlane
        zc_ref[0] = zc
        v_ref[0] = v
        mixed = ws_ref[...] * v + bs_ref[...]
        yd = u * mixed
        yd_ref[0] = (yd * _group_rsqrt(yd, GROUP_W) * gout_ref[:, 768:1024]).astype(BF16)
    d = wsum / cnt - zc
    yc = jnp.dot(d.astype(BF16), wpool_ref[...], preferred_element_type=F32) * pscale_ref[...]
    yc_ref[0] = (yc * _group_rsqrt(yc, GROUP_W) * gout_ref[:, 512:768]).astype(BF16)


def _inproj(x, sh1, sc1, lw, pos0, pool_buf=None):
    sample = pool_buf is not None
    nb, t, d = x.shape
    tm = t if sample else min(512, t)
    nt = t // tm
    half = D_ROPE // 2
    freqs = ROPE_BASE ** (-jnp.arange(half, dtype=F32) / half)
    pos = (jnp.full((t,), pos0, I32) if sample else pos0 + jnp.arange(t, dtype=I32)).astype(F32)
    ang = pos[:, None] * freqs[None, :]
    cos, sin = jnp.cos(ang), jnp.sin(ang)
    cos4 = jnp.tile(jnp.concatenate([cos, cos], -1), (1, 4))
    sin4 = jnp.tile(jnp.concatenate([-sin, sin], -1), (1, 4))
    if sample:
        ws = jnp.repeat(lw['w_s'][:, 0, 0], D_VD)[None, :]
        bs = jnp.repeat(lw['b_s'][:, 0], D_VD)[None, :]
        pool = pool_buf
        pool_spec = pl.BlockSpec(pool.shape, lambda b, i: (0, 0, 0))
        ws_spec = pl.BlockSpec((1, 256), lambda b, i: (0, 0))
        bs_spec = pl.BlockSpec((1, 256), lambda b, i: (0, 0))
        tail = tm
    else:
        ws, bs = lw['ws_cat'], lw['bs_full']
        pool = jnp.zeros((1, 8, LANE), F32)
        pool_spec = pl.BlockSpec(pool.shape, lambda b, i: (0, 0, 0))
        ws_spec = pl.BlockSpec((CHUNK, 4 * CHUNK), lambda b, i: (0, 0))
        bs_spec = pl.BlockSpec((CHUNK, 256), lambda b, i: (0, 0))
        tail = 16
    nbk = max(tm // BLK, 8)

    def row(w):
        return pl.BlockSpec((1, tm, w), lambda b, i: (b, i, 0))

    def const(shape):
        return pl.BlockSpec(shape, lambda b, i: (0,) * len(shape))

    def tailspec(w):
        return pl.BlockSpec((1, tail, w), lambda b, i: (b, 0, 0))

    mod = row(d) if sample else pl.BlockSpec((1, 1, d), lambda b, i: (b, 0, 0))
    in_specs = [row(d), mod, mod,
                const((1, d)), const((d, N_PAD)),
                pl.BlockSpec((tm, LANE), lambda b, i: (i, 0)), pl.BlockSpec((tm, LANE), lambda b, i: (i, 0)),
                const((256, 512)), const((1, 512)), const((1, LANE)), const((1, LANE)), const((1, 256)),
                const((1, LANE)), const((1, LANE)), const((1, LANE)), const((BLK, LANE)), const((256, 256)),
                const((1, 256)), const((1, 256)), ws_spec, bs_spec, const((1, d)), pool_spec]
    outs = [('qm', 1024, BF16), ('kh', 256, BF16), ('cl', LANE, F32), ('cv', 256, BF16), ('kr', D_ROPE, F32),
            ('qn', 256, F32), ('gt', LANE, F32), ('cmp', LANE, F32), ('sel', LANE, F32), ('selb', 256, BF16),
            ('win', LANE, F32), ('winb', 256, BF16)]
    out_shape = [jax.ShapeDtypeStruct((nb, t, w), dt) for _, w, dt in outs]
    out_specs = [row(w) for _, w, _ in outs]
    out_shape += [jax.ShapeDtypeStruct((nb, nt * nbk, LANE), F32)]
    out_specs += [pl.BlockSpec((1, nbk, LANE), lambda b, i: (b, i, 0))]
    out_shape += [jax.ShapeDtypeStruct((nb, tail, 256), F32), jax.ShapeDtypeStruct((nb, tail, 256), F32)]
    out_specs += [tailspec(256), tailspec(256)]
    out_shape += [jax.ShapeDtypeStruct((nb, t, 256), BF16), jax.ShapeDtypeStruct((nb, t, 256), BF16)]
    out_specs += [row(256), row(256)]
    res = pl.pallas_call(
        functools.partial(_inproj_kernel, sample, tm),
        out_shape=out_shape,
        grid=(nb, nt),
        in_specs=in_specs,
        out_specs=out_specs,
        scratch_shapes=[pltpu.VMEM((16, 256), F32)],
        compiler_params=_cparams(("arbitrary", "arbitrary")),
        name="inproj_sample" if sample else "inproj_prompt",
    )(x, sh1, sc1, lw['norm1'], lw['w_pad'], cos4, sin4, lw['wuk_bd'], lw['gq_mla'], lw['g_kv'], lw['g_k_mla'],
      lw['gq_nsa'], lw['g_k_cmp'], lw['g_k_sel'], lw['g_k_win'], lw['wc'], lw['wpool_bd'], lw['pool_scale'],
      lw['g_sgu'], ws, bs, lw['g_out'], pool)
    names = [n for n, _, _ in outs] + ['kvc', 'zc_tail', 'v_tail', 'yc', 'yd']
    return dict(zip(names, res))


def _block_diag(blocks):
    n = len(blocks)
    r, c = blocks[0].shape
    stacked = jnp.stack(blocks)
    eye = jnp.eye(n, dtype=stacked.dtype)
    return (eye[:, None, :, None] * stacked[:, :, None, :]).reshape(n * r, n * c)


def _pad_lanes(v, n=LANE):
    return jnp.concatenate([v, jnp.ones((n - v.shape[0],), v.dtype)])[None, :]


def _layer_weights(l, w):
    lw = {}
    w_in = jnp.concatenate([w['w_in'][l], jnp.zeros((D_MODEL, 1), F32)], axis=1)
    lw['w_pad'] = jnp.take(w_in, _IN_COLS, axis=1).astype(BF16)
    lw['norm1'] = w['norm1'][l][None, :]
    lw['norm2'] = w['norm2'][l][None, :]
    lw['wuk_bd'] = _block_diag([w['w_uk'][l][h] for h in range(H_A)]).astype(BF16)
    lw['wuv_bd'] = _block_diag([w['w_uv'][l][h] for h in range(H_A)]).astype(BF16)
    lw['wuv_wide'] = jnp.concatenate([w['w_uv'][l][h] for h in range(H_A)], axis=1).astype(BF16)
    lw['gq_mla'] = w['g_q_mla'][l].reshape(1, H_A * D_LAT)
    lw['g_kv'] = w['g_kv'][l][None, :]
    lw['g_k_mla'] = w['g_k_mla'][l][None, :]
    lw['gq_nsa'] = jnp.tile(w['g_q_nsa'][l], H_B)[None, :]
    lw['g_k_cmp'] = _pad_lanes(w['g_k_cmp'][l])
    lw['g_k_sel'] = _pad_lanes(w['g_k_sel'][l])
    lw['g_k_win'] = _pad_lanes(w['g_k_win'][l])
    lw['wc'] = jnp.concatenate([jnp.tile(w['w_cmp_k'][l][:, None], (1, D_NSA)),
                                jnp.tile(w['w_cmp_v'][l][:, None], (1, D_NSA))], axis=1)
    lw['wpool_bd'] = _block_diag([w['w_pool'][l][g] for g in range(4)]).astype(BF16)
    lw['pool_scale'] = w['pool_scale'][l][None, :]
    lw['g_sgu'] = w['g_sgu'][l][None, :]
    tril = jnp.tril(jnp.ones((CHUNK, CHUNK), F32))
    lw['ws_cat'] = jnp.concatenate([w['w_s'][l][h] * tril for h in range(H_D)], axis=1).astype(BF16)
    lw['bs_full'] = jnp.repeat(jnp.transpose(w['b_s'][l]), D_VD, axis=1)
    lw['w_s'] = w['w_s'][l]
    lw['b_s'] = w['b_s'][l]
    lw['g_out'] = w['g_out'][l][None, :]
    lw['w_out'] = w['w_out'][l].astype(BF16)
    lw['w_router'] = jnp.pad(w['w_router'][l], ((0, 0), (0, LANE - N_EXP)))
    lw['b_router'] = jnp.concatenate([w['b_router'][l], jnp.full((LANE - N_EXP,), NEG_INF, F32)])[None, :]
    lw['moe'] = dict(layer=l, w_gu=w['w_gu'], b_gu=w['b_gu'], w_down=w['w_down'], b_down=w['b_down'])
    return lw


def _causal_pairs(nq):
    qi = np.concatenate([np.full((q + 1,), q, np.int32) for q in range(nq)])
    ki = np.concatenate([np.arange(q + 1, dtype=np.int32) for q in range(nq)])
    return jnp.asarray(qi), jnp.asarray(ki)


def _flash_step(s, v_ext, m_ref, acc_ref):
    m_prev = m_ref[...]
    m_new = jnp.maximum(m_prev, jnp.max(s, -1, keepdims=True))
    m_use = jnp.maximum(m_new, 0.1 * NEG_INF)
    p = jnp.exp(s - jnp.concatenate([m_use] * (s.shape[1] // LANE), axis=1))
    alpha = jnp.exp(m_prev - m_new)
    acc_ref[...] = (jnp.concatenate([alpha, alpha], axis=1) * acc_ref[...] +
                    jnp.dot(p.astype(BF16), v_ext, preferred_element_type=F32))
    m_ref[...] = m_new


def _mla_kernel(tq, qi_ref, ki_ref, q_ref, k_ref, v_ref, wuv_ref, gout_ref, o_ref, q4_ref, m_ref, acc_ref):
    p_id = pl.program_id(1)
    qi, ki = qi_ref[p_id], ki_ref[p_id]

    @pl.when(ki == 0)
    def _():
        for hh in range(H_A):
            q4_ref[hh * tq:(hh + 1) * tq, :] = q_ref[0, :, hh * 256:(hh + 1) * 256]
        m_ref[...] = jnp.full_like(m_ref, NEG_INF)
        acc_ref[...] = jnp.zeros_like(acc_ref)

    def update(masked):
        s = lax.dot_general(q4_ref[...], k_ref[0], (((1,), (1,)), ((), ())), preferred_element_type=F32)
        if masked:
            row = lax.broadcasted_iota(I32, (tq, tq), 0)
            col = lax.broadcasted_iota(I32, (tq, tq), 1)
            keep = jnp.concatenate([col <= row] * H_A, axis=0)
            s = jnp.where(keep, s, NEG_INF)
        _flash_step(s, v_ref[0], m_ref, acc_ref)

    @pl.when(ki < qi)
    def _():
        update(False)

    @pl.when(ki == qi)
    def _():
        update(True)
        acc = acc_ref[...]
        o = acc[:, :LANE] / acc[:, LANE:]
        ocat = jnp.concatenate([o[hh * tq:(hh + 1) * tq] for hh in range(H_A)], axis=1)
        ya = jnp.dot(ocat.astype(BF16), wuv_ref[...], preferred_element_type=F32)
        o_ref[0] = (ya * _group_rsqrt(ya, GROUP_W) * gout_ref[:, 0:256]).astype(BF16)


def _mla_prompt(qm, kh, cv, lw):
    nb, t, _ = qm.shape
    tq = min(512, t)
    nq = t // tq
    qi, ki = _causal_pairs(nq)
    grid_spec = pltpu.PrefetchScalarGridSpec(
        num_scalar_prefetch=2,
        grid=(nb, qi.shape[0]),
        in_specs=[pl.BlockSpec((1, tq, 1024), lambda b, p, qi, ki: (b, qi[p], 0)),
                  pl.BlockSpec((1, tq, 256), lambda b, p, qi, ki: (b, ki[p], 0)),
                  pl.BlockSpec((1, tq, 256), lambda b, p, qi, ki: (b, ki[p], 0)),
                  pl.BlockSpec((512, 256), lambda b, p, qi, ki: (0, 0)),
                  pl.BlockSpec((1, D_MODEL), lambda b, p, qi, ki: (0, 0))],
        out_specs=pl.BlockSpec((1, tq, 256), lambda b, p, qi, ki: (b, qi[p], 0)),
        scratch_shapes=[pltpu.VMEM((H_A * tq, 256), BF16), pltpu.VMEM((H_A * tq, LANE), F32),
                        pltpu.VMEM((H_A * tq, 256), F32)])
    return pl.pallas_call(
        functools.partial(_mla_kernel, tq),
        out_shape=jax.ShapeDtypeStruct((nb, t, 256), BF16),
        grid_spec=grid_spec,
        compiler_params=_cparams(("arbitrary", "arbitrary")),
        name="mla_prompt",
    )(qi, ki, qm, kh, cv, lw['wuv_bd'], lw['g_out'])


def _topk_mask(imp, jb, n_blocks, k):
    sel = jnp.zeros(imp.shape, F32)
    for _ in range(k):
        m = jnp.max(imp, -1, keepdims=True)
        idx = jnp.min(jnp.where(imp == m, jb, n_blocks), -1, keepdims=True)
        pick = jb == idx
        sel = jnp.where(pick, 1.0, sel)
        imp = jnp.where(pick, -3e38, imp)
    return sel


def _masked_softmax(s, keep):
    s = jnp.where(keep, s, NEG_INF)
    m = jnp.max(s, -1, keepdims=True)
    e = jnp.where(keep, jnp.exp(s - m), 0.0)
    return e / jnp.maximum(jnp.sum(e, -1, keepdims=True), 1e-30)


def _merge_heads(parts):
    lane = _lane_iota((1, LANE))
    lo = lane < D_NSA
    pair = [jnp.where(lo, pltpu.roll(parts[2 * j], D_NSA, 1), parts[2 * j + 1]) for j in range(2)]
    return jnp.concatenate(pair, axis=1)


def _head_cols(g, offset):
    lane2 = _lane_iota((1, 256))
    out = jnp.zeros((g.shape[0], 256), F32)
    for hh in range(H_B):
        col = g[:, 3 * hh + offset:3 * hh + offset + 1]
        out = jnp.where((lane2 >= hh * D_NSA) & (lane2 < (hh + 1) * D_NSA), col, out)
    return out


def _nsa_kernel(tq, n_blocks, wt, qi_ref, ki_ref, qn_ref, gt_ref, kvc_ref, selb_ref, winb_ref, gout_ref, o_ref,
                q4_ref, selm_ref, oc_ref, ms_ref, as_ref, mw_ref, aw_ref):
    p_id = pl.program_id(1)
    qi, ki = qi_ref[p_id], ki_ref[p_id]
    qpos = qi * tq + lax.broadcasted_iota(I32, (tq, 1), 0)

    @pl.when(ki == 0)
    def _():
        q = qn_ref[0]
        kvc = kvc_ref[0]
        kc = kvc[:, :D_NSA]
        vc = kvc[:, D_NSA:].astype(BF16)
        jb = lax.broadcasted_iota(I32, (1, n_blocks), 1)
        cpos = (jb + 1) * BLK - 1
        valid = cpos <= qpos
        dist = (qpos - cpos).astype(F32)
        imp = jnp.zeros((tq, n_blocks), F32)
        for hh in range(H_B):
            qh = q[:, hh * D_NSA:(hh + 1) * D_NSA]
            s = lax.dot_general(qh, kc, (((1,), (1,)), ((), ())), precision=HI, preferred_element_type=F32)
            pc = _masked_softmax(s - ALIBI[hh] * dist, valid)
            oc_ref[:, hh * D_NSA:(hh + 1) * D_NSA] = jnp.dot(pc.astype(BF16), vc, preferred_element_type=F32)
            imp = imp + pc
            q4_ref[hh * tq:(hh + 1) * tq, :] = jnp.concatenate(
                [qh, jnp.zeros((tq, LANE - D_NSA), F32)], axis=1).astype(BF16)
        cur = qpos // BLK
        imp = jnp.where((jb == 0) | (jb == cur), FORCE_SCORE, jnp.where(jb > cur, -1.0, imp))
        selm_ref[...] = _topk_mask(imp, jb, n_blocks, min(N_SEL, n_blocks)).astype(BF16)
        for m_r, a_r in ((ms_ref, as_ref), (mw_ref, aw_ref)):
            m_r[...] = jnp.full_like(m_r, NEG_INF)
            a_r[...] = jnp.zeros_like(a_r)

    kpos = ki * tq + lax.broadcasted_iota(I32, (1, tq), 1)
    disti = qpos - kpos
    kposf = kpos.astype(F32)

    def branch(kv, allowed, m_r, a_r):
        pen = jnp.where(allowed, 0.0, NEG_INF)
        s = lax.dot_general(q4_ref[...], kv[:, :LANE], (((1,), (1,)), ((), ())), preferred_element_type=F32)
        s = jnp.concatenate([s[hh * tq:(hh + 1) * tq] + (pen + ALIBI[hh] * kposf) for hh in range(H_B)], axis=0)
        _flash_step(s, kv, m_r, a_r)

    expand = (lax.broadcasted_iota(I32, (n_blocks, tq), 0) ==
              (ki * tq + lax.broadcasted_iota(I32, (n_blocks, tq), 1)) // BLK).astype(BF16)
    chosen = jnp.dot(selm_ref[...], expand, preferred_element_type=F32) > 0.5
    branch(selb_ref[0], chosen & (disti >= 0), ms_ref, as_ref)

    @pl.when(ki >= qi - wt)
    def _():
        branch(winb_ref[0], (disti >= 0) & (disti < WINDOW), mw_ref, aw_ref)

    @pl.when(ki == qi)
    def _():
        a_s, a_w = as_ref[...], aw_ref[...]
        o_s = a_s[:, :LANE] / jnp.maximum(a_s[:, LANE:], 1e-30)
        o_w = a_w[:, :LANE] / jnp.maximum(a_w[:, LANE:], 1e-30)
        g = gt_ref[0]
        ys = _merge_heads([o_s[hh * tq:(hh + 1) * tq] for hh in range(H_B)])
        yw = _merge_heads([o_w[hh * tq:(hh + 1) * tq] for hh in range(H_B)])
        y = _head_cols(g, 0) * oc_ref[...] + _head_cols(g, 1) * ys + _head_cols(g, 2) * yw
        o_ref[0] = (y * _group_rsqrt(y, GROUP_W) * gout_ref[:, 256:512]).astype(BF16)


def _nsa_prompt(qn, gt, kvc, selb, winb, lw):
    nb, t, _ = qn.shape
    tq = min(512, t)
    nq = t // tq
    n_blocks = t // BLK
    wt = WINDOW // tq
    qi, ki = _causal_pairs(nq)

    def qmap(b, p, qi, ki):
        return (b, qi[p], 0)

    def kmap(b, p, qi, ki):
        return (b, ki[p], 0)

    def wmap(b, p, qi, ki):
        return (b, jnp.maximum(ki[p], qi[p] - wt), 0)

    grid_spec = pltpu.PrefetchScalarGridSpec(
        num_scalar_prefetch=2,
        grid=(nb, qi.shape[0]),
        in_specs=[pl.BlockSpec((1, tq, 256), qmap), pl.BlockSpec((1, tq, LANE), qmap),
                  pl.BlockSpec((1, n_blocks, LANE), lambda b, p, qi, ki: (b, 0, 0)),
                  pl.BlockSpec((1, tq, 256), kmap), pl.BlockSpec((1, tq, 256), wmap),
                  pl.BlockSpec((1, D_MODEL), lambda b, p, qi, ki: (0, 0))],
        out_specs=pl.BlockSpec((1, tq, 256), qmap),
        scratch_shapes=[pltpu.VMEM((H_B * tq, LANE), BF16), pltpu.VMEM((tq, n_blocks), BF16),
                        pltpu.VMEM((tq, 256), F32),
                        pltpu.VMEM((H_B * tq, LANE), F32), pltpu.VMEM((H_B * tq, 256), F32),
                        pltpu.VMEM((H_B * tq, LANE), F32), pltpu.VMEM((H_B * tq, 256), F32)])
    return pl.pallas_call(
        functools.partial(_nsa_kernel, tq, n_blocks, wt),
        out_shape=jax.ShapeDtypeStruct((nb, t, 256), BF16),
        grid_spec=grid_spec,
        compiler_params=_cparams(("arbitrary", "arbitrary")),
        name="nsa_prompt",
    )(qi, ki, qn, gt, kvc, selb, winb, lw['g_out'])


TOK_ROWS = D_MODEL // LANE


def _store_token_tiles(ref, x):
    n = x.shape[0]
    for j in range(TOK_ROWS):
        ref[pl.ds(j, n, stride=TOK_ROWS), :] = x[:, j * LANE:(j + 1) * LANE]


def _load_token_tiles(ref, n):
    return jnp.concatenate([ref[pl.ds(j, n, stride=TOK_ROWS), :] for j in range(TOK_ROWS)], axis=1)


def _outproj_kernel(ya_ref, yb_ref, yc_ref, yd_ref, x_ref, g1_ref, sc_ref, sh_ref, n2_ref, wo_ref, wr_ref, br_ref,
                    x1_ref, h2_ref, ti_ref, tw_ref):
    mix = jnp.zeros(x_ref.shape, F32)
    for g, y_ref in enumerate((ya_ref, yb_ref, yc_ref, yd_ref)):
        mix = mix + jnp.dot(y_ref[...].astype(BF16), wo_ref[g * GROUP_W:(g + 1) * GROUP_W, :],
                            preferred_element_type=F32)
    x1 = x_ref[...] + g1_ref[0] * mix
    x1_ref[...] = x1
    h2 = x1 * lax.rsqrt(jnp.mean(x1 * x1, -1, keepdims=True) + EPS) * n2_ref[...]
    h2 = h2 * (1.0 + sc_ref[0]) + sh_ref[0]
    _store_token_tiles(h2_ref, h2)
    logits = jnp.dot(h2, wr_ref[...], precision=HI, preferred_element_type=F32) + br_ref[...]
    lane = _lane_iota((1, LANE))
    ti = jnp.zeros(logits.shape, I32)
    tv = jnp.full(logits.shape, NEG_INF, F32)
    for k in range(TOP_K):
        m = jnp.max(logits, -1, keepdims=True)
        idx = jnp.min(jnp.where(logits == m, lane, LANE), -1, keepdims=True)
        ti = jnp.where(lane == k, idx, ti)
        tv = jnp.where(lane == k, m, tv)
        logits = jnp.where(lane == idx, -3e38, logits)
    e = jnp.where(lane < TOP_K, jnp.exp(tv - jnp.max(tv, -1, keepdims=True)), 0.0)
    ti_ref[...] = ti
    tw_ref[...] = e / jnp.sum(e, -1, keepdims=True)


def _outproj(ys, x, g1, sc2, sh2, lw, rows_per_mod):
    n, d = x.shape
    tm = min(512, n)
    per_row = rows_per_mod == 1
    if per_row:
        mod = pl.BlockSpec((1, tm, d), lambda i: (0, i, 0))
        g1, sc2, sh2 = [a.reshape(1, n, d) for a in (g1, sc2, sh2)]
    else:
        mod = pl.BlockSpec((1, 1, d), lambda i: (i * tm // rows_per_mod, 0, 0))
        g1, sc2, sh2 = [a.reshape(-1, 1, d) for a in (g1, sc2, sh2)]

    def row(w):
        return pl.BlockSpec((tm, w), lambda i: (i, 0))

    def const(shape):
        return pl.BlockSpec(shape, lambda i: (0, 0))

    return pl.pallas_call(
        _outproj_kernel,
        out_shape=[jax.ShapeDtypeStruct((n, d), F32), jax.ShapeDtypeStruct((n * TOK_ROWS, LANE), F32),
                   jax.ShapeDtypeStruct((n, LANE), I32), jax.ShapeDtypeStruct((n, LANE), F32)],
        grid=(n // tm,),
        in_specs=[row(256)] * 4 + [row(d), mod, mod, mod, const((1, d)), const((d, d)), const((d, LANE)),
                                    const((1, LANE))],
        out_specs=[row(d), pl.BlockSpec((tm * TOK_ROWS, LANE), lambda i: (i, 0)), row(LANE), row(LANE)],
        compiler_params=_cparams(("arbitrary",)),
        name="outproj",
    )(*ys, x, g1, sc2, sh2, lw['norm2'], lw['w_out'], lw['w_router'], lw['b_router'])


def _start_token_gather(idx_ref, n, src_hbm, buf, sem, straight_line=False):
    def body(j, carry):
        r = idx_ref[0, 0, j]
        pltpu.make_async_copy(src_hbm.at[pl.ds(pl.multiple_of(r * TOK_ROWS, TOK_ROWS), TOK_ROWS)],
                              buf.at[pl.ds(pl.multiple_of(j * TOK_ROWS, TOK_ROWS), TOK_ROWS)], sem).start()
        return carry

    if straight_line:
        for j in range(n):
            body(j, 0)
    else:
        lax.fori_loop(0, n, body, 0, unroll=8)


def _wait_token_gather(n, src_hbm, buf, sem):
    pltpu.make_async_copy(src_hbm.at[pl.ds(0, n * TOK_ROWS)], buf, sem).wait()


MOE_TM = 512


def _moe_kernel(te_ref, tv_ref, idx0_ref, idxn_ref, h2_hbm, wgu_ref, bgu_ref, wd_ref, bd_ref, o_ref,
                wgu_bf, wd_bf, xbuf, sem):
    i = pl.program_id(0)
    n_tiles = pl.num_programs(0)
    prev = te_ref[jnp.maximum(i - 1, 0)]
    slot = i % 2

    @pl.when((i == 0) & (tv_ref[0] == 1))
    def _():
        _start_token_gather(idx0_ref, MOE_TM, h2_hbm, xbuf.at[0], sem.at[0])

    nxt = jnp.minimum(i + 1, n_tiles - 1)
    nxt_valid = (i + 1 < n_tiles) & (tv_ref[nxt] == 1)

    @pl.when((i == 0) | (te_ref[i] != prev))
    def _():
        for c in range(8):
            wgu_bf[c * 128:(c + 1) * 128, :] = wgu_ref[0, 0, c * 128:(c + 1) * 128, :].astype(BF16)
            wd_bf[c * 128:(c + 1) * 128, :] = wd_ref[0, 0, c * 128:(c + 1) * 128, :].astype(BF16)

    @pl.when(nxt_valid)
    def _():
        _wait_token_gather(MOE_TM, h2_hbm, xbuf.at[slot], sem.at[slot])
        _start_token_gather(idxn_ref, MOE_TM, h2_hbm, xbuf.at[1 - slot], sem.at[1 - slot], straight_line=True)
        _moe_tile(xbuf.at[slot], wgu_bf, bgu_ref, wd_bf, bd_ref, o_ref)

    @pl.when((tv_ref[i] == 1) & jnp.logical_not(nxt_valid))
    def _():
        _wait_token_gather(MOE_TM, h2_hbm, xbuf.at[slot], sem.at[slot])
        _moe_tile(xbuf.at[slot], wgu_bf, bgu_ref, wd_bf, bd_ref, o_ref)

    @pl.when(tv_ref[i] == 0)
    def _():
        o_ref[...] = jnp.zeros_like(o_ref)


def _moe_tile(x_tiles, wgu_bf, bgu_ref, wd_bf, bd_ref, o_ref):
    x = _load_token_tiles(x_tiles, MOE_TM).astype(BF16)
    hu = jnp.dot(x, wgu_bf[...], preferred_element_type=F32) + bgu_ref[0, 0]
    gt = jnp.minimum(hu[:, :D_FF], SWIGLU_LIMIT)
    up = jnp.clip(hu[:, D_FF:], -SWIGLU_LIMIT, SWIGLU_LIMIT)
    act = gt * jax.nn.sigmoid(SWIGLU_ALPHA * gt) * (up + 1.0)
    _store_token_tiles(o_ref, jnp.dot(act.astype(BF16), wd_bf[...], preferred_element_type=F32) + bd_ref[0, 0])


def _moe_ffn(h2_all, tok_of_row, tile_expert, tile_valid, layer, w_gu, b_gu, w_down, b_down):
    d = D_MODEL
    depth = w_gu.shape[0]
    n_tiles = tok_of_row.shape[0] // MOE_TM
    idx = tok_of_row.reshape(n_tiles, 1, MOE_TM)
    grid_spec = pltpu.PrefetchScalarGridSpec(
        num_scalar_prefetch=2,
        grid=(n_tiles,),
        in_specs=[pl.BlockSpec((1, 1, MOE_TM), lambda i, te, tv: (0, 0, 0), memory_space=pltpu.SMEM),
                  pl.BlockSpec((1, 1, MOE_TM), lambda i, te, tv: (jnp.minimum(i + 1, n_tiles - 1), 0, 0),
                               memory_space=pltpu.SMEM),
                  pl.BlockSpec(memory_space=pl.ANY),
                  pl.BlockSpec((1, 1, d, 2 * D_FF), lambda i, te, tv: (layer, te[i], 0, 0)),
                  pl.BlockSpec((1, 1, 1, 2 * D_FF), lambda i, te, tv: (layer, te[i], 0, 0)),
                  pl.BlockSpec((1, 1, D_FF, d), lambda i, te, tv: (layer, te[i], 0, 0)),
                  pl.BlockSpec((1, 1, 1, d), lambda i, te, tv: (layer, te[i], 0, 0))],
        out_specs=pl.BlockSpec((MOE_TM * TOK_ROWS, LANE), lambda i, te, tv: (i, 0)),
        scratch_shapes=[pltpu.VMEM((d, 2 * D_FF), BF16), pltpu.VMEM((D_FF, d), BF16),
                        pltpu.VMEM((2, MOE_TM * TOK_ROWS, LANE), F32), pltpu.SemaphoreType.DMA((2,))])
    return pl.pallas_call(
        _moe_kernel,
        out_shape=jax.ShapeDtypeStruct((n_tiles * MOE_TM * TOK_ROWS, LANE), F32),
        grid_spec=grid_spec,
        compiler_params=_cparams(("arbitrary",)),
        name="moe_ffn",
    )(tile_expert, tile_valid, idx, idx, h2_all, w_gu, b_gu.reshape(depth, N_EXP, 1, 2 * D_FF), w_down,
      b_down.reshape(depth, N_EXP, 1, d))


def _combine_kernel(tm, idx0_ref, idxn_ref, x1_ref, tw_ref, g2_ref, ys_hbm, o_ref, ybuf, sem):
    i = pl.program_id(0)
    n_tiles = pl.num_programs(0)
    slot = i % 2
    n_rows = TOP_K * tm

    @pl.when(i == 0)
    def _():
        _start_token_gather(idx0_ref, n_rows, ys_hbm, ybuf.at[0], sem.at[0])

    @pl.when(i + 1 < n_tiles)
    def _():
        _start_token_gather(idxn_ref, n_rows, ys_hbm, ybuf.at[1 - slot], sem.at[1 - slot])

    _wait_token_gather(n_rows, ys_hbm, ybuf.at[slot], sem.at[slot])
    tw = tw_ref[...]
    acc = jnp.zeros(x1_ref.shape, F32)
    for k in range(TOP_K):
        yk = _load_token_tiles(ybuf.at[slot, pl.ds(k * tm * TOK_ROWS, tm * TOK_ROWS)], tm)
        acc = acc + tw[:, k:k + 1] * yk
    o_ref[...] = x1_ref[...] + g2_ref[0] * acc


COMBINE_TM = 256


def _combine(x1, ys, rows, tw, g2, rows_per_mod):
    n, d = x1.shape
    tm = min(COMBINE_TM, n)
    n_tiles = n // tm
    idx = jnp.transpose(rows.reshape(n_tiles, tm, TOP_K), (0, 2, 1)).reshape(n_tiles, 1, TOP_K * tm)
    per_row = rows_per_mod == 1
    if per_row:
        mod = pl.BlockSpec((1, tm, d), lambda i: (0, i, 0))
        g2 = g2.reshape(1, n, d)
    else:
        mod = pl.BlockSpec((1, 1, d), lambda i: (i * tm // rows_per_mod, 0, 0))
        g2 = g2.reshape(-1, 1, d)
    return pl.pallas_call(
        functools.partial(_combine_kernel, tm),
        out_shape=jax.ShapeDtypeStruct((n, d), F32),
        grid=(n_tiles,),
        in_specs=[pl.BlockSpec((1, 1, TOP_K * tm), lambda i: (0, 0, 0), memory_space=pltpu.SMEM),
                  pl.BlockSpec((1, 1, TOP_K * tm), lambda i: (jnp.minimum(i + 1, n_tiles - 1), 0, 0),
                               memory_space=pltpu.SMEM),
                  pl.BlockSpec((tm, d), lambda i: (i, 0)), pl.BlockSpec((tm, LANE), lambda i: (i, 0)), mod,
                  pl.BlockSpec(memory_space=pl.ANY)],
        out_specs=pl.BlockSpec((tm, d), lambda i: (i, 0)),
        scratch_shapes=[pltpu.VMEM((2, TOP_K * tm * TOK_ROWS, LANE), F32), pltpu.SemaphoreType.DMA((2,))],
        compiler_params=_cparams(("arbitrary",)),
        name="combine",
    )(idx, idx, x1, tw, g2, ys)


def _moe(h2_all, ti_all, lw_moe):
    n = h2_all.shape[0] // TOK_ROWS
    pairs = n * TOP_K
    e = ti_all[:, :TOP_K].reshape(pairs)
    onehot = (e[:, None] == jnp.arange(N_EXP, dtype=I32)[None, :]).astype(I32)
    csum = jnp.cumsum(onehot, axis=0)
    rank = jnp.take_along_axis(csum, e[:, None], axis=1)[:, 0] - 1
    counts = csum[-1]
    padded = (counts + MOE_TM - 1) // MOE_TM * MOE_TM
    ends = jnp.cumsum(padded)
    starts = ends - padded
    row = starts[e] + rank
    r_max = -(-pairs // MOE_TM) * MOE_TM + N_EXP * MOE_TM
    tok_of_row = jnp.zeros((r_max,), I32).at[row].set(jnp.arange(pairs, dtype=I32) // TOP_K, unique_indices=True)
    tile_start = jnp.arange(r_max // MOE_TM, dtype=I32) * MOE_TM
    tile_valid = (tile_start < ends[-1]).astype(I32)
    tile_expert = jnp.minimum(jnp.sum((tile_start[:, None] >= ends[None, :]).astype(I32), axis=1), N_EXP - 1)
    tile_expert = jnp.where(tile_valid == 1, tile_expert, jnp.max(jnp.where(counts > 0, jnp.arange(N_EXP), 0)))
    ys = _moe_ffn(h2_all, tok_of_row, tile_expert, tile_valid, lw_moe['layer'], lw_moe['w_gu'], lw_moe['b_gu'],
                  lw_moe['w_down'], lw_moe['b_down'])
    return ys, row.reshape(n, TOP_K)


DEC_PAGES = 16
DEC_GROUP = 2


def _rows_to_headmajor(o8, width):
    wide = jnp.concatenate([o8] * H_B, axis=1)
    row = lax.broadcasted_iota(I32, wide.shape, 0)
    lane = lax.broadcasted_iota(I32, wide.shape, 1)
    return jnp.sum(jnp.where(row == lane // width, wide, 0.0), axis=0, keepdims=True)


def _head_rows(q_row, width, pad_to):
    rows = [q_row[:, hh * width:(hh + 1) * width] for hh in range(H_B)]
    q4 = jnp.concatenate(rows + [jnp.zeros((8 - H_B, width), F32)], axis=0)
    if pad_to > width:
        q4 = jnp.concatenate([q4, jnp.zeros((8, pad_to - width), F32)], axis=1)
    return q4


def _slope_rows():
    row = lax.broadcasted_iota(I32, (8, 1), 0)
    out = jnp.zeros((8, 1), F32)
    for hh in range(H_B):
        out = jnp.where(row == hh, ALIBI[hh], out)
    return out


def _nt(a, b, **kw):
    return lax.dot_general(a, b, (((1,), (1,)), ((), ())), preferred_element_type=F32, **kw)


def _decode_stream_kernel(layer, n_pages, pg, pt_ref, qm_ref, kh_ref, cv_ref, gk_ref, wc_ref, wuvw_ref,
                          gout_ref, c_hbm, krt_hbm, cmp_hbm, ya_ref, ksum_ref,
                          cbuf, krbuf, cmpbuf, sem):
    step = pl.program_id(0)
    n_steps = pl.num_programs(0)
    n_chunks = n_pages // pg
    grp = range(DEC_GROUP)

    def copies(st, chunk, slot):
        out = []
        for g in grp:
            for j in range(pg):
                page = pt_ref[st * DEC_GROUP + g, chunk * pg + j]
                rows = pl.ds(j * LANE, LANE)
                out.append(pltpu.make_async_copy(c_hbm.at[layer, page], cbuf.at[g, slot, rows], sem.at[0, slot]))
                out.append(pltpu.make_async_copy(krt_hbm.at[layer, page], krbuf.at[g, slot, j], sem.at[1, slot]))
                out.append(pltpu.make_async_copy(cmp_hbm.at[layer, page], cmpbuf.at[g, slot, rows], sem.at[2, slot]))
        return out

    @pl.when(step == 0)
    def _():
        for cp in copies(0, 0, 0):
            cp.start()

    ones_c = jnp.ones((8, D_LAT), BF16)
    gk = gk_ref[...]
    wc = wc_ref[...]

    init, qs = [], []
    for g in grp:
        q32 = qm_ref[g].astype(F32)
        qlat = jnp.concatenate([q32[:, hh * 256:hh * 256 + D_LAT] for hh in range(H_A)] +
                               [jnp.zeros((8 - H_A, D_LAT), F32)], axis=0)
        qrot = jnp.concatenate([q32[:, hh * 256 + D_LAT + hh * D_ROPE:hh * 256 + D_LAT + (hh + 1) * D_ROPE]
                                for hh in range(H_A)] + [jnp.zeros((8 - H_A, D_ROPE), F32)], axis=0)
        kh_new = kh_ref[g].astype(F32)
        m0 = (jnp.sum(qlat * kh_new[:, :D_LAT], -1, keepdims=True) +
              jnp.sum(qrot * kh_new[:, D_LAT:D_LAT + D_ROPE], -1, keepdims=True))
        init += [m0, jnp.ones((8, 1), F32), jnp.broadcast_to(cv_ref[g][:, :D_LAT].astype(F32), (8, D_LAT))]
        qs.append((qlat.astype(BF16), qrot.astype(BF16)))

    def body(chunk, carry):
        slot = (step * n_chunks + chunk) % 2
        last = chunk + 1 == n_chunks
        nxt_s = jnp.where(last, step + 1, step)
        nxt_c = jnp.where(last, 0, chunk + 1)

        @pl.when(nxt_s < n_steps)
        def _():
            for cp in copies(nxt_s, nxt_c, 1 - slot):
                cp.start()

        for cp in copies(step, chunk, slot):
            cp.wait()
        out = []
        for g in grp:
            m_prev, l_prev, acc = carry[3 * g:3 * g + 3]
            qlat_b, qrot_b = qs[g]
            c = cbuf[g, slot]
            krt = krbuf[g, slot]
            s_rot = jnp.concatenate([jnp.dot(qrot_b, krt[j].astype(BF16), preferred_element_type=F32)
                                     for j in range(pg)], axis=1)
            ss_rot = jnp.concatenate([jnp.sum(krt[j] * krt[j], axis=0, keepdims=True) for j in range(pg)], axis=1)
            ss = _nt(ones_c, (c * c).astype(BF16)) + ss_rot
            r = lax.rsqrt(ss * (1.0 / MLA_DIM) + EPS)
            s = (_nt(qlat_b, (c * gk).astype(BF16)) + s_rot) * r
            m_new = jnp.maximum(m_prev, jnp.max(s, -1, keepdims=True))
            p = jnp.exp(s - m_new)
            alpha = jnp.exp(m_prev - m_new)
            l_new = alpha * l_prev + jnp.sum(p, -1, keepdims=True)
            acc = alpha * acc + jnp.dot(p.astype(BF16), c.astype(BF16), preferred_element_type=F32)
            cm = cmpbuf[g, slot]
            ksum = jnp.sum(cm.reshape(2 * pg, BLK, LANE) * wc[None], axis=1)
            ksum_ref[g, pl.ds(pl.multiple_of(chunk * 2 * pg, 2 * pg), 2 * pg), :] = ksum
            out += [m_new, l_new, acc]
        return tuple(out)

    fin = lax.fori_loop(0, n_chunks, body, tuple(init))
    for g in grp:
        m_f, l_f, acc = fin[3 * g:3 * g + 3]
        o = acc / l_f
        yw = jnp.dot(o.astype(BF16), wuvw_ref[...], preferred_element_type=F32)
        row = lax.broadcasted_iota(I32, yw.shape, 0)
        lane2 = lax.broadcasted_iota(I32, yw.shape, 1)
        ya = jnp.sum(jnp.where(row == lane2 // D_VA, yw, 0.0), axis=0, keepdims=True)
        ya_ref[g] = ya * _group_rsqrt(ya, GROUP_W) * gout_ref[:, 0:256]


def _decode_stream(layer, page_table, qm, kh, cv, lw, cache_c, cache_krt, cache_cmp):
    db, n_pages = page_table.shape
    n_blk = 2 * n_pages
    pg = min(DEC_PAGES, n_pages)

    grp = DEC_GROUP
    assert db % grp == 0

    def per(w):
        return pl.BlockSpec((grp, 1, w), lambda b, pt: (b, 0, 0))

    def const(shape):
        return pl.BlockSpec(shape, lambda b, pt: (0,) * len(shape))

    anyspec = pl.BlockSpec(memory_space=pl.ANY)
    grid_spec = pltpu.PrefetchScalarGridSpec(
        num_scalar_prefetch=1,
        grid=(db // grp,),
        in_specs=[per(1024), per(256), per(256), const((1, LANE)), const((BLK, LANE)),
                  const((LANE, 256)), const((1, D_MODEL)), anyspec, anyspec, anyspec],
        out_specs=[per(256), pl.BlockSpec((grp, n_blk, LANE), lambda b, pt: (b, 0, 0))],
        scratch_shapes=[pltpu.VMEM((grp, 2, pg * LANE, LANE), F32), pltpu.VMEM((grp, 2, pg, D_ROPE, LANE), F32),
                        pltpu.VMEM((grp, 2, pg * LANE, LANE), F32), pltpu.SemaphoreType.DMA((3, 2))])
    ya, ksum = pl.pallas_call(
        functools.partial(_decode_stream_kernel, layer, n_pages, pg),
        out_shape=[jax.ShapeDtypeStruct((db, 1, 256), F32), jax.ShapeDtypeStruct((db, n_blk, LANE), F32)],
        grid_spec=grid_spec,
        compiler_params=_cparams(("arbitrary",)),
        name="decode_stream",
    )(page_table, qm.reshape(db, 1, 1024), kh.reshape(db, 1, 256), cv.reshape(db, 1, 256),
      lw['g_k_mla'], lw['wc'], lw['wuv_wide'], lw['g_out'], cache_c, cache_krt, cache_cmp)
    return ya.reshape(db, 256), ksum


CMP_GROUP = 8


def _cmp_select_kernel(n_blk, past, ksum_ref, qn_ref, gcmp_ref, oc_ref, sel_ref):
    lane = _lane_iota((1, LANE))
    khalf = lane < D_NSA
    jb = lax.broadcasted_iota(I32, (1, n_blk), 1)
    cpos = (jb + 1) * BLK - 1
    bias = _slope_rows() * (past - cpos).astype(F32)
    ocs, imps = [], []
    for i in range(CMP_GROUP):
        ksum = ksum_ref[i]
        ssq = jnp.sum(jnp.where(khalf, ksum * ksum, 0.0), -1, keepdims=True)
        kvc = jnp.where(khalf, ksum * lax.rsqrt(ssq * (1.0 / D_NSA) + EPS) * gcmp_ref[...], ksum)
        q8 = _head_rows(qn_ref[i:i + 1, :], D_NSA, D_NSA)
        s_c = _nt(q8, kvc[:, :D_NSA], precision=HI) - bias
        e_c = jnp.exp(s_c - jnp.max(s_c, -1, keepdims=True))
        p_c = e_c / jnp.maximum(jnp.sum(e_c, -1, keepdims=True), 1e-30)
        oc8 = jnp.dot(p_c.astype(BF16), kvc[:, D_NSA:].astype(BF16), preferred_element_type=F32)
        ocs.append(_rows_to_headmajor(oc8, D_NSA))
        hrow = lax.broadcasted_iota(I32, p_c.shape, 0)
        imps.append(jnp.sum(jnp.where(hrow < H_B, p_c, 0.0), axis=0, keepdims=True))
    oc_ref[...] = jnp.concatenate(ocs, axis=0)
    imp = jnp.where(jb == 0, FORCE_SCORE, jnp.concatenate(imps, axis=0))
    sel = jnp.zeros((CMP_GROUP, LANE), I32)
    for k in range(N_SEL - 1):
        mx = jnp.max(imp, -1, keepdims=True)
        idx = jnp.min(jnp.where(imp == mx, jb, n_blk), -1, keepdims=True)
        sel = jnp.where(lane == k, idx, sel)
        imp = jnp.where(jb == idx, -3e38, imp)
    sel_ref[...] = sel


def _cmp_select(ksum, qn, lw, past):
    db, n_blk, _ = ksum.shape
    return pl.pallas_call(
        functools.partial(_cmp_select_kernel, n_blk, past),
        out_shape=[jax.ShapeDtypeStruct((db, 256), F32), jax.ShapeDtypeStruct((db, LANE), I32)],
        grid=(db // CMP_GROUP,),
        in_specs=[pl.BlockSpec((CMP_GROUP, n_blk, LANE), lambda g: (g, 0, 0)),
                  pl.BlockSpec((CMP_GROUP, 256), lambda g: (g, 0)),
                  pl.BlockSpec((1, LANE), lambda g: (0, 0))],
        out_specs=[pl.BlockSpec((CMP_GROUP, 256), lambda g: (g, 0)),
                   pl.BlockSpec((CMP_GROUP, LANE), lambda g: (g, 0))],
        compiler_params=_cparams(("arbitrary",)),
        name="cmp_select",
    )(ksum, qn, lw['g_k_cmp'])


def _decode_sel_kernel(layer, n_pages, w_buf, pt_ref, si_ref, qn_ref, gt_ref, oc_ref, snew_ref, wnew_ref, win_ref,
                       gout_ref, sel_hbm, yb_ref, selbuf, sem):
    b = pl.program_id(0)
    n_samples = pl.num_programs(0)
    n_sel = N_SEL - 1
    past = n_pages * LANE
    slot = b % 2

    def copies(bb, sl):
        out = []
        for r in range(n_sel):
            j = si_ref[bb, r]
            page = pt_ref[bb, j // 2]
            off = pl.multiple_of((j % 2) * BLK, BLK)
            out.append(pltpu.make_async_copy(sel_hbm.at[layer, page, pl.ds(off, BLK)],
                                             selbuf.at[sl, pl.ds(r * BLK, BLK)], sem.at[sl]))
        return out

    @pl.when(b == 0)
    def _():
        for cp in copies(0, 0):
            cp.start()

    @pl.when(b + 1 < n_samples)
    def _():
        for cp in copies(b + 1, 1 - slot):
            cp.start()

    q8 = _head_rows(qn_ref[0], D_NSA, LANE)
    q8b = q8.astype(BF16)
    q8r = q8b.astype(F32)
    slope = _slope_rows()
    lane = _lane_iota((1, LANE))

    def branch(kv, dist, valid, new_row):
        kvb = kv.astype(BF16)
        s = _nt(q8b, kvb) - slope * dist
        if valid is not None:
            s = jnp.where(valid, s, NEG_INF)
        newb = new_row.astype(BF16).astype(F32)
        s_new = jnp.sum(jnp.where(lane < D_NSA, q8r * newb, 0.0), -1, keepdims=True)
        m = jnp.maximum(jnp.max(s, -1, keepdims=True), s_new)
        e = jnp.exp(s - m)
        if valid is not None:
            e = jnp.where(valid, e, 0.0)
        e_new = jnp.exp(s_new - m)
        den = jnp.sum(e, -1, keepdims=True) + e_new
        o = (jnp.dot(e.astype(BF16), kvb, preferred_element_type=F32) + e_new * newb) / den
        return _rows_to_headmajor(pltpu.roll(o, D_NSA, 1)[:, :D_NSA], D_NSA)

    iw = lax.broadcasted_iota(I32, (1, w_buf), 1)
    distw = w_buf - iw
    validw = (distw < WINDOW) & (past - w_buf + iw >= 0)
    yw = branch(win_ref[0, 0], distw.astype(F32), validw, wnew_ref[0])

    for cp in copies(b, slot):
        cp.wait()
    ls = lax.broadcasted_iota(I32, (1, n_sel * BLK), 1)
    spos = ls % BLK
    for r in range(n_sel):
        spos = spos + jnp.where(ls // BLK == r, si_ref[b, r] * BLK, 0)
    ys = branch(selbuf[slot], (past - spos).astype(F32), None, snew_ref[0])
    g = gt_ref[0]
    y = _head_cols(g, 0) * oc_ref[0] + _head_cols(g, 1) * ys + _head_cols(g, 2) * yw
    yb_ref[0] = y * _group_rsqrt(y, GROUP_W) * gout_ref[:, 256:512]


def _decode_sel(layer, page_table, sel_idx, qn, gt, oc, sel_new, win_new, win_cache, lw, cache_sel):
    db, n_pages = page_table.shape
    w_buf = win_cache.shape[2]

    def per(w):
        return pl.BlockSpec((1, 1, w), lambda b, pt, si: (b, 0, 0))

    grid_spec = pltpu.PrefetchScalarGridSpec(
        num_scalar_prefetch=2,
        grid=(db,),
        in_specs=[per(256), per(LANE), per(256), per(LANE), per(LANE),
                  pl.BlockSpec((1, 1, w_buf, LANE), lambda b, pt, si: (layer, b, 0, 0)),
                  pl.BlockSpec((1, D_MODEL), lambda b, pt, si: (0, 0)),
                  pl.BlockSpec(memory_space=pl.ANY)],
        out_specs=per(256),
        scratch_shapes=[pltpu.VMEM((2, (N_SEL - 1) * BLK, LANE), F32), pltpu.SemaphoreType.DMA((2,))])
    yb = pl.pallas_call(
        functools.partial(_decode_sel_kernel, layer, n_pages, w_buf),
        out_shape=jax.ShapeDtypeStruct((db, 1, 256), F32),
        grid_spec=grid_spec,
        compiler_params=_cparams(("arbitrary",)),
        name="decode_sel",
    )(page_table, sel_idx, qn.reshape(db, 1, 256), gt.reshape(db, 1, LANE), oc.reshape(db, 1, 256),
      sel_new.reshape(db, 1, LANE),
      win_new.reshape(db, 1, LANE), win_cache, lw['g_out'], cache_sel)
    return yb.reshape(db, 256)


def kernel(x_prompt, x_sample, cache_mla_c, cache_mla_kr, cache_nsa_cmp_kv, cache_nsa_sel_kv, cache_nsa_win_kv,
           state_pool, page_table, c_prompt, c_sample, w_ada, b_ada, norm1, norm2, w_in, w_uk, w_uv, g_kv, g_q_mla,
           g_k_mla, g_q_nsa, g_k_cmp, g_k_sel, g_k_win, w_cmp_k, w_cmp_v, w_pool, pool_scale, g_sgu, w_s, b_s,
           g_out, w_out, w_router, b_router, w_gu, b_gu, w_down, b_down):
    w = dict(norm1=norm1, norm2=norm2, w_in=w_in, w_uk=w_uk, w_uv=w_uv, g_kv=g_kv, g_q_mla=g_q_mla, g_k_mla=g_k_mla,
             g_q_nsa=g_q_nsa, g_k_cmp=g_k_cmp, g_k_sel=g_k_sel, g_k_win=g_k_win, w_cmp_k=w_cmp_k, w_cmp_v=w_cmp_v,
             w_pool=w_pool, pool_scale=pool_scale, g_sgu=g_sgu, w_s=w_s, b_s=b_s, g_out=g_out, w_out=w_out,
             w_router=w_router, b_router=b_router, w_gu=w_gu, b_gu=b_gu, w_down=w_down, b_down=b_down)
    nb, t, d = x_prompt.shape
    db = x_sample.shape[0]
    depth = w_ada.shape[0]
    n_tok = nb * t
    past = page_table.shape[1] * cache_mla_c.shape[2]
    w_keep = min(WINDOW, t)

    rows = -(-(nb + db) // 8) * 8
    c_all = jnp.concatenate([c_prompt, c_sample, jnp.zeros((rows - nb - db, d), F32)], axis=0)
    ada = _ada(c_all, w_ada, b_ada)
    cache_krt = jnp.swapaxes(cache_mla_kr, 2, 3)

    xp = x_prompt
    xs = x_sample.reshape(1, db, d)
    st = [[] for _ in range(13)]
    for l in range(depth):
        lw = _layer_weights(l, w)
        sh1p, sc1p, g1p, sh2p, sc2p, g2p = [z[:, None, :] for z in jnp.split(ada[l, :nb], 6, axis=-1)]
        sh1s, sc1s, g1s, sh2s, sc2s, g2s = jnp.split(ada[l, nb:nb + db], 6, axis=-1)

        o = _inproj(xp, sh1p, sc1p, lw, 0)
        ya = _mla_prompt(o['qm'], o['kh'], o['cv'], lw)
        yb = _nsa_prompt(o['qn'], o['gt'], o['kvc'], o['selb'], o['winb'], lw)
        ys = [y.reshape(n_tok, GROUP_W) for y in (ya, yb, o['yc'], o['yd'])]
        x1p, h2p, tip, twp = _outproj(ys, xp.reshape(n_tok, d), g1p, sc2p, sh2p, lw, t)

        s = _inproj(xs, sh1s[None], sc1s[None], lw, past, pool_buf=state_pool[l])
        ya_s, ksum_s = _decode_stream(l, page_table, s['qm'][0], s['kh'][0], s['cv'][0], lw,
                                      cache_mla_c, cache_krt, cache_nsa_cmp_kv)
        oc_s, sel_s = _cmp_select(ksum_s, s['qn'][0], lw, past)
        yb_s = _decode_sel(l, page_table, sel_s, s['qn'][0], s['gt'][0], oc_s, s['sel'][0],
                           s['win'][0], cache_nsa_win_kv, lw, cache_nsa_sel_kv)
        x1s, h2s, tis, tws = _outproj([ya_s, yb_s, s['yc'][0], s['yd'][0]], xs[0], g1s, sc2s, sh2s, lw, 1)

        ys_moe, rows = _moe(jnp.concatenate([h2p, h2s], axis=0), jnp.concatenate([tip, tis], axis=0), lw['moe'])
        xp = _combine(x1p, ys_moe, rows[:n_tok], twp, g2p, t).reshape(nb, t, d)
        xs = _combine(x1s, ys_moe, rows[n_tok:], tws, g2s, 1).reshape(1, db, d)

        per_l = (o['cl'], s['cl'][0][:, None, :], o['kr'], s['kr'][0][:, None, :], o['cmp'], s['cmp'][0][:, None, :],
                 o['sel'], s['sel'][0][:, None, :], o['win'][:, t - w_keep:],
                 jnp.concatenate([cache_nsa_win_kv[l][:, 1:], s['win'][0][:, None, :]], axis=1),
                 o['zc_tail'][:, 16 - POOL_PAD:],
                 jnp.concatenate([state_pool[l][:, 1:], s['zc_tail'][0][:, None, :]], axis=1),
                 s['v_tail'][0][:, None, :])
        for k, v in enumerate(per_l):
            st[k].append(v)
    return (xp, xs.reshape(db, 1, d)) + tuple(jnp.stack(v, axis=0) for v in st)
```

```python
import functools

import numpy as np
import jax
import jax.numpy as jnp
from jax import lax
from jax.experimental import pallas as pl
from jax.experimental.pallas import tpu as pltpu

F32 = jnp.float32
BF16 = jnp.bfloat16
I32 = jnp.int32
HI = lax.Precision.HIGHEST

EPS = 1e-6
NEG_INF = -1e30
D_MODEL = 1024
GROUP_W = 256
H_A, D_NOPE, D_ROPE, D_LAT, D_VA = 4, 64, 32, 128, 64
MLA_DIM = D_LAT + D_ROPE
MLA_SCALE = (D_NOPE + D_ROPE) ** -0.5
ROPE_BASE = 10000.0
H_B, D_NSA, BLK, N_SEL, WINDOW = 4, 64, 64, 16, 512
NSA_SCALE = D_NSA ** -0.5
FORCE_SCORE = 1e4
POOL_WINDOWS = (2, 4, 8, 16)
POOL_PAD = 15
H_D, CHUNK, D_VD = 4, 128, 64
N_EXP, TOP_K, D_FF = 32, 4, 1024
SWIGLU_LIMIT, SWIGLU_ALPHA = 7.0, 1.702
ALIBI = tuple(2.0 ** (-8.0 * (i + 1) / H_B) for i in range(H_B))

LANE = 128
VMEM_LIMIT = 56 * 1024 * 1024

C_QN, C_QR, C_QRS, C_CA, C_KR, C_KRS, C_QB, C_CMP, C_SEL, C_WIN, C_ZG, C_ZC, C_ZD, N_PAD = (
    0, 256, 384, 512, 640, 768, 896, 1152, 1280, 1408, 1536, 1664, 1920, 2432)


def _in_cols():
    qa, ca, ra, qb = 0, 384, 512, 544
    cmp_, sel, win, zg, zc, zd, n_in = 800, 928, 1056, 1184, 1196, 1452, 1964
    per = D_NOPE + D_ROPE
    half = D_ROPE // 2
    cols = []
    for h in range(H_A):
        cols += list(range(qa + h * per, qa + h * per + D_NOPE))
    for h in range(H_A):
        cols += list(range(qa + h * per + D_NOPE, qa + (h + 1) * per))
    for h in range(H_A):
        b = qa + h * per + D_NOPE
        cols += list(range(b + half, b + D_ROPE)) + list(range(b, b + half))
    cols += list(range(ca, ca + D_LAT))
    cols += list(range(ra, ra + D_ROPE)) * 4
    cols += (list(range(ra + half, ra + D_ROPE)) + list(range(ra, ra + half))) * 4
    cols += list(range(qb, qb + 256))
    cols += list(range(cmp_, cmp_ + 128)) + list(range(sel, sel + 128)) + list(range(win, win + 128))
    cols += list(range(zg, zg + 12)) + [n_in] * (LANE - 12)
    cols += list(range(zc, zc + 256))
    cols += list(range(zd, zd + 512))
    assert len(cols) == N_PAD
    return np.asarray(cols, np.int32)


_IN_COLS = _in_cols()


def _cparams(sem):
    return pltpu.CompilerParams(dimension_semantics=sem, vmem_limit_bytes=VMEM_LIMIT)


def _lane_iota(shape):
    return lax.broadcasted_iota(I32, shape, len(shape) - 1)


def _group_rsqrt(x, width):
    n = x.shape[-1]
    sq = x * x
    if width == n:
        return lax.rsqrt(jnp.sum(sq, -1, keepdims=True) * (1.0 / width) + EPS)
    lane = _lane_iota((1, n))
    out = jnp.zeros_like(x)
    for g in range(n // width):
        m = (lane >= g * width) & (lane < (g + 1) * width)
        ss = jnp.sum(jnp.where(m, sq, 0.0), -1, keepdims=True)
        out = jnp.where(m, lax.rsqrt(ss * (1.0 / width) + EPS), out)
    return out


def _ada_kernel(c_ref, w_ref, b_ref, o_ref):
    c = c_ref[...]
    s = c * jax.nn.sigmoid(c)
    o_ref[0] = jnp.dot(s.astype(BF16), w_ref[0].astype(BF16), preferred_element_type=F32) + b_ref[0]


def _ada(c_all, w_ada, b_ada):
    depth, d, n = w_ada.shape
    rows = c_all.shape[0]
    tn = 1536
    return pl.pallas_call(
        _ada_kernel,
        out_shape=jax.ShapeDtypeStruct((depth, rows, n), F32),
        grid=(depth, n // tn),
        in_specs=[pl.BlockSpec((rows, d), lambda l, j: (0, 0)),
                  pl.BlockSpec((1, d, tn), lambda l, j: (l, 0, j)),
                  pl.BlockSpec((1, 1, tn), lambda l, j: (l, 0, j))],
        out_specs=pl.BlockSpec((1, rows, tn), lambda l, j: (l, 0, j)),
        compiler_params=_cparams(("arbitrary", "arbitrary")),
        name="ada",
    )(c_all, w_ada, b_ada.reshape(depth, 1, n))


def _gelu_tanh(x):
    return 0.5 * x * (1.0 + jnp.tanh(0.7978845608028654 * (x + 0.044715 * (x * x * x))))


def _inproj_kernel(sample, tm, x_ref, sh_ref, sc_ref, n1_ref, w_ref, cos_ref, sin_ref, wuk_ref, gq_ref,
                   gkv_ref, gk_ref, gqn_ref, gcmp_ref, gsel_ref, gwin_ref, wc_ref, wpool_ref, pscale_ref,
                   gsgu_ref, ws_ref, bs_ref, gout_ref, pool_ref,
                   qm_ref, kh_ref, cl_ref, cv_ref, kr_ref, qn_ref, gt_ref, cmp_ref, sel_ref, selb_ref,
                   win_ref, winb_ref, kvc_ref, zc_ref, v_ref, yc_ref, yd_ref, carry_ref):
    i = pl.program_id(1)
    x = x_ref[0]
    xn = x * lax.rsqrt(jnp.mean(x * x, -1, keepdims=True) + EPS) * n1_ref[...]
    h = xn * (1.0 + sc_ref[0]) + sh_ref[0]
    z = jnp.dot(h.astype(BF16), w_ref[...], preferred_element_type=F32)
    cos4, sin4 = cos_ref[...], sin_ref[...]
    lane = _lane_iota((1, LANE))

    qlat = jnp.dot(z[:, C_QN:C_QN + 256].astype(BF16), wuk_ref[...], preferred_element_type=F32)
    qrot = z[:, C_QR:C_QR + LANE] * cos4 + z[:, C_QRS:C_QRS + LANE] * sin4
    for hh in range(H_A):
        lat = qlat[:, hh * D_LAT:(hh + 1) * D_LAT]
        rot = jnp.where((lane >= hh * D_ROPE) & (lane < (hh + 1) * D_ROPE), qrot, 0.0)
        ss = jnp.sum(lat * lat, -1, keepdims=True) + jnp.sum(rot * rot, -1, keepdims=True)
        r = lax.rsqrt(ss * (1.0 / MLA_DIM) + EPS) * MLA_SCALE
        qm_ref[0, :, hh * 256:hh * 256 + D_LAT] = (lat * r * gq_ref[:, hh * D_LAT:(hh + 1) * D_LAT]).astype(BF16)
        qm_ref[0, :, hh * 256 + D_LAT:(hh + 1) * 256] = (rot * r).astype(BF16)

    ca = z[:, C_CA:C_CA + LANE]
    c_lat = ca * _group_rsqrt(ca, LANE) * gkv_ref[...]
    kr4 = z[:, C_KR:C_KR + LANE] * cos4 + z[:, C_KRS:C_KRS + LANE] * sin4
    cl_ref[0] = c_lat
    ones_blk = jnp.ones((tm, LANE), BF16)
    cv_ref[0, :, :LANE] = c_lat.astype(BF16)
    cv_ref[0, :, LANE:] = ones_blk
    kr_ref[0] = kr4[:, :D_ROPE]
    ssk = jnp.sum(c_lat * c_lat, -1, keepdims=True) + 0.25 * jnp.sum(kr4 * kr4, -1, keepdims=True)
    rk = lax.rsqrt(ssk * (1.0 / MLA_DIM) + EPS)
    kh_ref[0, :, :LANE] = (c_lat * rk * gk_ref[...]).astype(BF16)
    kh_ref[0, :, LANE:] = (kr4 * rk).astype(BF16)

    zqb = z[:, C_QB:C_QB + 256]
    qn_ref[0] = zqb * _group_rsqrt(zqb, D_NSA) * gqn_ref[...] * NSA_SCALE
    gt_ref[0] = jax.nn.sigmoid(z[:, C_ZG:C_ZG + LANE])
    cmp_rows = z[:, C_CMP:C_CMP + LANE]
    cmp_ref[0] = cmp_rows
    khalf = lane < D_NSA

    def norm_k(zz, g_ref):
        ss = jnp.sum(jnp.where(khalf, zz * zz, 0.0), -1, keepdims=True)
        return jnp.where(khalf, zz * lax.rsqrt(ss * (1.0 / D_NSA) + EPS) * g_ref[...], zz)

    sel_rows = norm_k(z[:, C_SEL:C_SEL + LANE], gsel_ref)
    win_rows = norm_k(z[:, C_WIN:C_WIN + LANE], gwin_ref)
    sel_ref[0] = sel_rows
    selb_ref[0, :, :LANE] = sel_rows.astype(BF16)
    selb_ref[0, :, LANE:] = ones_blk
    win_ref[0] = win_rows
    winb_ref[0, :, :LANE] = win_rows.astype(BF16)
    winb_ref[0, :, LANE:] = ones_blk

    zc = z[:, C_ZC:C_ZC + 256]
    zd = z[:, C_ZD:C_ZD + 512]
    uv = _gelu_tanh(zd)
    u = uv[:, :256]
    v = uv[:, 256:]
    v = v * _group_rsqrt(v, D_VD) * gsgu_ref[...]
    lane2 = _lane_iota((1, 256))
    wsel = [lane2 < 64, lane2 < 128, lane2 < 192]

    def pick(a2, a4, a8, a16):
        return jnp.where(wsel[0], a2, jnp.where(wsel[1], a4, jnp.where(wsel[2], a8, a16)))

    wlane = pick(2.0, 4.0, 8.0, 16.0)
    if not sample:
        nbk = tm // BLK
        ksum = jnp.sum(cmp_rows.reshape(nbk, BLK, LANE) * wc_ref[...][None], axis=1)
        kvc_ref[0] = norm_k(ksum, gcmp_ref)

        @pl.when(i == 0)
        def _():
            carry_ref[...] = jnp.zeros_like(carry_ref)

        ext = jnp.concatenate([carry_ref[...], zc], axis=0)
        a2 = ext + pltpu.roll(ext, 1, 0)
        a4 = a2 + pltpu.roll(a2, 2, 0)
        a8 = a4 + pltpu.roll(a4, 4, 0)
        a16 = a8 + pltpu.roll(a8, 8, 0)
        wsum = pick(a2, a4, a8, a16)[16:]
        carry_ref[...] = zc[tm - 16:]
        pos = (i * tm + lax.broadcasted_iota(I32, (tm, 1), 0)).astype(F32)
        cnt = jnp.minimum(pos + 1.0, wlane)
        zc_ref[0] = zc[tm - 16:]
        v_ref[0] = v[tm - 16:]
        for c in range(tm // CHUNK):
            vc = v[c * CHUNK:(c + 1) * CHUNK]
            vst = jnp.concatenate(
                [jnp.where((lane2 >= hh * D_VD) & (lane2 < (hh + 1) * D_VD), vc, 0.0) for hh in range(H_D)], axis=0)
            mixed = jnp.dot(ws_ref[...], vst.astype(BF16), preferred_element_type=F32) + bs_ref[...]
            yd = u[c * CHUNK:(c + 1) * CHUNK] * mixed
            yd_ref[0, c * CHUNK:(c + 1) * CHUNK, :] = (
                yd * _group_rsqrt(yd, GROUP_W) * gout_ref[:, 768:1024]).astype(BF16)
    else:
        kvc_ref[0] = jnp.zeros_like(kvc_ref[0])
        pb = pool_ref[...]
        sums = []
        for w in POOL_WINDOWS:
            sums.append(zc + jnp.sum(pb[:, POOL_PAD - (w - 1):, :], axis=1))
        wsum = pick(*sums)
        cnt = wlane
        zc_ref[0] = zc
        v_ref[0] = v
        mixed = ws_ref[...] * v + bs_ref[...]
        yd = u * mixed
        yd_ref[0] = (yd * _group_rsqrt(yd, GROUP_W) * gout_ref[:, 768:1024]).astype(BF16)
    d = wsum / cnt - zc
    yc = jnp.dot(d.astype(BF16), wpool_ref[...], preferred_element_type=F32) * pscale_ref[...]
    yc_ref[0] = (yc * _group_rsqrt(yc, GROUP_W) * gout_ref[:, 512:768]).astype(BF16)


def _inproj(x, sh1, sc1, lw, pos0, pool_buf=None):
    sample = pool_buf is not None
    nb, t, d = x.shape
    tm = t if sample else min(512, t)
    nt = t // tm
    half = D_ROPE // 2
    freqs = ROPE_BASE ** (-jnp.arange(half, dtype=F32) / half)
    pos = (jnp.full((t,), pos0, I32) if sample else pos0 + jnp.arange(t, dtype=I32)).astype(F32)
    ang = pos[:, None] * freqs[None, :]
    cos, sin = jnp.cos(ang), jnp.sin(ang)
    cos4 = jnp.tile(jnp.concatenate([cos, cos], -1), (1, 4))
    sin4 = jnp.tile(jnp.concatenate([-sin, sin], -1), (1, 4))
    if sample:
        ws = jnp.repeat(lw['w_s'][:, 0, 0], D_VD)[None, :]
        bs = jnp.repeat(lw['b_s'][:, 0], D_VD)[None, :]
        pool = pool_buf
        pool_spec = pl.BlockSpec(pool.shape, lambda b, i: (0, 0, 0))
        ws_spec = pl.BlockSpec((1, 256), lambda b, i: (0, 0))
        bs_spec = pl.BlockSpec((1, 256), lambda b, i: (0, 0))
        tail = tm
    else:
        ws, bs = lw['ws_cat'], lw['bs_full']
        pool = jnp.zeros((1, 8, LANE), F32)
        pool_spec = pl.BlockSpec(pool.shape, lambda b, i: (0, 0, 0))
        ws_spec = pl.BlockSpec((CHUNK, 4 * CHUNK), lambda b, i: (0, 0))
        bs_spec = pl.BlockSpec((CHUNK, 256), lambda b, i: (0, 0))
        tail = 16
    nbk = max(tm // BLK, 8)

    def row(w):
        return pl.BlockSpec((1, tm, w), lambda b, i: (b, i, 0))

    def const(shape):
        return pl.BlockSpec(shape, lambda b, i: (0,) * len(shape))

    def tailspec(w):
        return pl.BlockSpec((1, tail, w), lambda b, i: (b, 0, 0))

    mod = row(d) if sample else pl.BlockSpec((1, 1, d), lambda b, i: (b, 0, 0))
    in_specs = [row(d), mod, mod,
                const((1, d)), const((d, N_PAD)),
                pl.BlockSpec((tm, LANE), lambda b, i: (i, 0)), pl.BlockSpec((tm, LANE), lambda b, i: (i, 0)),
                const((256, 512)), const((1, 512)), const((1, LANE)), const((1, LANE)), const((1, 256)),
                const((1, LANE)), const((1, LANE)), const((1, LANE)), const((BLK, LANE)), const((256, 256)),
                const((1, 256)), const((1, 256)), ws_spec, bs_spec, const((1, d)), pool_spec]
    outs = [('qm', 1024, BF16), ('kh', 256, BF16), ('cl', LANE, F32), ('cv', 256, BF16), ('kr', D_ROPE, F32),
            ('qn', 256, F32), ('gt', LANE, F32), ('cmp', LANE, F32), ('sel', LANE, F32), ('selb', 256, BF16),
            ('win', LANE, F32), ('winb', 256, BF16)]
    out_shape = [jax.ShapeDtypeStruct((nb, t, w), dt) for _, w, dt in outs]
    out_specs = [row(w) for _, w, _ in outs]
    out_shape += [jax.ShapeDtypeStruct((nb, nt * nbk, LANE), F32)]
    out_specs += [pl.BlockSpec((1, nbk, LANE), lambda b, i: (b, i, 0))]
    out_shape += [jax.ShapeDtypeStruct((nb, tail, 256), F32), jax.ShapeDtypeStruct((nb, tail, 256), F32)]
    out_specs += [tailspec(256), tailspec(256)]
    out_shape += [jax.ShapeDtypeStruct((nb, t, 256), BF16), jax.ShapeDtypeStruct((nb, t, 256), BF16)]
    out_specs += [row(256), row(256)]
    res = pl.pallas_call(
        functools.partial(_inproj_kernel, sample, tm),
        out_shape=out_shape,
        grid=(nb, nt),
        in_specs=in_specs,
        out_specs=out_specs,
        scratch_shapes=[pltpu.VMEM((16, 256), F32)],
        compiler_params=_cparams(("arbitrary", "arbitrary")),
        name="inproj_sample" if sample else "inproj_prompt",
    )(x, sh1, sc1, lw['norm1'], lw['w_pad'], cos4, sin4, lw['wuk_bd'], lw['gq_mla'], lw['g_kv'], lw['g_k_mla'],
      lw['gq_nsa'], lw['g_k_cmp'], lw['g_k_sel'], lw['g_k_win'], lw['wc'], lw['wpool_bd'], lw['pool_scale'],
      lw['g_sgu'], ws, bs, lw['g_out'], pool)
    names = [n for n, _, _ in outs] + ['kvc', 'zc_tail', 'v_tail', 'yc', 'yd']
    return dict(zip(names, res))


def _block_diag(blocks):
    n = len(blocks)
    r, c = blocks[0].shape
    stacked = jnp.stack(blocks)
    eye = jnp.eye(n, dtype=stacked.dtype)
    return (eye[:, None, :, None] * stacked[:, :, None, :]).reshape(n * r, n * c)


def _pad_lanes(v, n=LANE):
    return jnp.concatenate([v, jnp.ones((n - v.shape[0],), v.dtype)])[None, :]


def _layer_weights(l, w):
    lw = {}
    w_in = jnp.concatenate([w['w_in'][l], jnp.zeros((D_MODEL, 1), F32)], axis=1)
    lw['w_pad'] = jnp.take(w_in, _IN_COLS, axis=1).astype(BF16)
    lw['norm1'] = w['norm1'][l][None, :]
    lw['norm2'] = w['norm2'][l][None, :]
    lw['wuk_bd'] = _block_diag([w['w_uk'][l][h] for h in range(H_A)]).astype(BF16)
    lw['wuv_bd'] = _block_diag([w['w_uv'][l][h] for h in range(H_A)]).astype(BF16)
    lw['wuv_wide'] = jnp.concatenate([w['w_uv'][l][h] for h in range(H_A)], axis=1).astype(BF16)
    lw['gq_mla'] = w['g_q_mla'][l].reshape(1, H_A * D_LAT)
    lw['g_kv'] = w['g_kv'][l][None, :]
    lw['g_k_mla'] = w['g_k_mla'][l][None, :]
    lw['gq_nsa'] = jnp.tile(w['g_q_nsa'][l], H_B)[None, :]
    lw['g_k_cmp'] = _pad_lanes(w['g_k_cmp'][l])
    lw['g_k_sel'] = _pad_lanes(w['g_k_sel'][l])
    lw['g_k_win'] = _pad_lanes(w['g_k_win'][l])
    lw['wc'] = jnp.concatenate([jnp.tile(w['w_cmp_k'][l][:, None], (1, D_NSA)),
                                jnp.tile(w['w_cmp_v'][l][:, None], (1, D_NSA))], axis=1)
    lw['wpool_bd'] = _block_diag([w['w_pool'][l][g] for g in range(4)]).astype(BF16)
    lw['pool_scale'] = w['pool_scale'][l][None, :]
    lw['g_sgu'] = w['g_sgu'][l][None, :]
    tril = jnp.tril(jnp.ones((CHUNK, CHUNK), F32))
    lw['ws_cat'] = jnp.concatenate([w['w_s'][l][h] * tril for h in range(H_D)], axis=1).astype(BF16)
    lw['bs_full'] = jnp.repeat(jnp.transpose(w['b_s'][l]), D_VD, axis=1)
    lw['w_s'] = w['w_s'][l]
    lw['b_s'] = w['b_s'][l]
    lw['g_out'] = w['g_out'][l][None, :]
    lw['w_out'] = w['w_out'][l].astype(BF16)
    lw['w_router'] = jnp.pad(w['w_router'][l], ((0, 0), (0, LANE - N_EXP)))
    lw['b_router'] = jnp.concatenate([w['b_router'][l], jnp.full((LANE - N_EXP,), NEG_INF, F32)])[None, :]
    lw['moe'] = dict(layer=l, w_gu=w['w_gu'], b_gu=w['b_gu'], w_down=w['w_down'], b_down=w['b_down'])
    return lw


def _causal_pairs(nq):
    qi = np.concatenate([np.full((q + 1,), q, np.int32) for q in range(nq)])
    ki = np.concatenate([np.arange(q + 1, dtype=np.int32) for q in range(nq)])
    return jnp.asarray(qi), jnp.asarray(ki)


def _flash_step(s, v_ext, m_ref, acc_ref):
    m_prev = m_ref[...]
    m_new = jnp.maximum(m_prev, jnp.max(s, -1, keepdims=True))
    m_use = jnp.maximum(m_new, 0.1 * NEG_INF)
    p = jnp.exp(s - jnp.concatenate([m_use] * (s.shape[1] // LANE), axis=1))
    alpha = jnp.exp(m_prev - m_new)
    acc_ref[...] = (jnp.concatenate([alpha, alpha], axis=1) * acc_ref[...] +
                    jnp.dot(p.astype(BF16), v_ext, preferred_element_type=F32))
    m_ref[...] = m_new


def _mla_kernel(tq, qi_ref, ki_ref, q_ref, k_ref, v_ref, wuv_ref, gout_ref, o_ref, q4_ref, m_ref, acc_ref):
    p_id = pl.program_id(1)
    qi, ki = qi_ref[p_id], ki_ref[p_id]

    @pl.when(ki == 0)
    def _():
        for hh in range(H_A):
            q4_ref[hh * tq:(hh + 1) * tq, :] = q_ref[0, :, hh * 256:(hh + 1) * 256]
        m_ref[...] = jnp.full_like(m_ref, NEG_INF)
        acc_ref[...] = jnp.zeros_like(acc_ref)

    def update(masked):
        s = lax.dot_general(q4_ref[...], k_ref[0], (((1,), (1,)), ((), ())), preferred_element_type=F32)
        if masked:
            row = lax.broadcasted_iota(I32, (tq, tq), 0)
            col = lax.broadcasted_iota(I32, (tq, tq), 1)
            keep = jnp.concatenate([col <= row] * H_A, axis=0)
            s = jnp.where(keep, s, NEG_INF)
        _flash_step(s, v_ref[0], m_ref, acc_ref)

    @pl.when(ki < qi)
    def _():
        update(False)

    @pl.when(ki == qi)
    def _():
        update(True)
        acc = acc_ref[...]
        o = acc[:, :LANE] / acc[:, LANE:]
        ocat = jnp.concatenate([o[hh * tq:(hh + 1) * tq] for hh in range(H_A)], axis=1)
        ya = jnp.dot(ocat.astype(BF16), wuv_ref[...], preferred_element_type=F32)
        o_ref[0] = (ya * _group_rsqrt(ya, GROUP_W) * gout_ref[:, 0:256]).astype(BF16)


def _mla_prompt(qm, kh, cv, lw):
    nb, t, _ = qm.shape
    tq = min(512, t)
    nq = t // tq
    qi, ki = _causal_pairs(nq)
    grid_spec = pltpu.PrefetchScalarGridSpec(
        num_scalar_prefetch=2,
        grid=(nb, qi.shape[0]),
        in_specs=[pl.BlockSpec((1, tq, 1024), lambda b, p, qi, ki: (b, qi[p], 0)),
                  pl.BlockSpec((1, tq, 256), lambda b, p, qi, ki: (b, ki[p], 0)),
                  pl.BlockSpec((1, tq, 256), lambda b, p, qi, ki: (b, ki[p], 0)),
                  pl.BlockSpec((512, 256), lambda b, p, qi, ki: (0, 0)),
                  pl.BlockSpec((1, D_MODEL), lambda b, p, qi, ki: (0, 0))],
        out_specs=pl.BlockSpec((1, tq, 256), lambda b, p, qi, ki: (b, qi[p], 0)),
        scratch_shapes=[pltpu.VMEM((H_A * tq, 256), BF16), pltpu.VMEM((H_A * tq, LANE), F32),
                        pltpu.VMEM((H_A * tq, 256), F32)])
    return pl.pallas_call(
        functools.partial(_mla_kernel, tq),
        out_shape=jax.ShapeDtypeStruct((nb, t, 256), BF16),
        grid_spec=grid_spec,
        compiler_params=_cparams(("arbitrary", "arbitrary")),
        name="mla_prompt",
    )(qi, ki, qm, kh, cv, lw['wuv_bd'], lw['g_out'])


def _topk_mask(imp, jb, n_blocks, k):
    sel = jnp.zeros(imp.shape, F32)
    for _ in range(k):
        m = jnp.max(imp, -1, keepdims=True)
        idx = jnp.min(jnp.where(imp == m, jb, n_blocks), -1, keepdims=True)
        pick = jb == idx
        sel = jnp.where(pick, 1.0, sel)
        imp = jnp.where(pick, -3e38, imp)
    return sel


def _masked_softmax(s, keep):
    s = jnp.where(keep, s, NEG_INF)
    m = jnp.max(s, -1, keepdims=True)
    e = jnp.where(keep, jnp.exp(s - m), 0.0)
    return e / jnp.maximum(jnp.sum(e, -1, keepdims=True), 1e-30)


def _merge_heads(parts):
    lane = _lane_iota((1, LANE))
    lo = lane < D_NSA
    pair = [jnp.where(lo, pltpu.roll(parts[2 * j], D_NSA, 1), parts[2 * j + 1]) for j in range(2)]
    return jnp.concatenate(pair, axis=1)


def _head_cols(g, offset):
    lane2 = _lane_iota((1, 256))
    out = jnp.zeros((g.shape[0], 256), F32)
    for hh in range(H_B):
        col = g[:, 3 * hh + offset:3 * hh + offset + 1]
        out = jnp.where((lane2 >= hh * D_NSA) & (lane2 < (hh + 1) * D_NSA), col, out)
    return out


def _nsa_kernel(tq, n_blocks, wt, qi_ref, ki_ref, qn_ref, gt_ref, kvc_ref, selb_ref, winb_ref, gout_ref, o_ref,
                q4_ref, selm_ref, oc_ref, ms_ref, as_ref, mw_ref, aw_ref):
    p_id = pl.program_id(1)
    qi, ki = qi_ref[p_id], ki_ref[p_id]
    qpos = qi * tq + lax.broadcasted_iota(I32, (tq, 1), 0)

    @pl.when(ki == 0)
    def _():
        q = qn_ref[0]
        kvc = kvc_ref[0]
        kc = kvc[:, :D_NSA]
        vc = kvc[:, D_NSA:].astype(BF16)
        jb = lax.broadcasted_iota(I32, (1, n_blocks), 1)
        cpos = (jb + 1) * BLK - 1
        valid = cpos <= qpos
        dist = (qpos - cpos).astype(F32)
        imp = jnp.zeros((tq, n_blocks), F32)
        for hh in range(H_B):
            qh = q[:, hh * D_NSA:(hh + 1) * D_NSA]
            s = lax.dot_general(qh, kc, (((1,), (1,)), ((), ())), precision=HI, preferred_element_type=F32)
            pc = _masked_softmax(s - ALIBI[hh] * dist, valid)
            oc_ref[:, hh * D_NSA:(hh + 1) * D_NSA] = jnp.dot(pc.astype(BF16), vc, preferred_element_type=F32)
            imp = imp + pc
            q4_ref[hh * tq:(hh + 1) * tq, :] = jnp.concatenate(
                [qh, jnp.zeros((tq, LANE - D_NSA), F32)], axis=1).astype(BF16)
        cur = qpos // BLK
        imp = jnp.where((jb == 0) | (jb == cur), FORCE_SCORE, jnp.where(jb > cur, -1.0, imp))
        selm_ref[...] = _topk_mask(imp, jb, n_blocks, min(N_SEL, n_blocks)).astype(BF16)
        for m_r, a_r in ((ms_ref, as_ref), (mw_ref, aw_ref)):
            m_r[...] = jnp.full_like(m_r, NEG_INF)
            a_r[...] = jnp.zeros_like(a_r)

    kpos = ki * tq + lax.broadcasted_iota(I32, (1, tq), 1)
    disti = qpos - kpos
    kposf = kpos.astype(F32)

    def branch(kv, allowed, m_r, a_r):
        pen = jnp.where(allowed, 0.0, NEG_INF)
        s = lax.dot_general(q4_ref[...], kv[:, :LANE], (((1,), (1,)), ((), ())), preferred_element_type=F32)
        s = jnp.concatenate([s[hh * tq:(hh + 1) * tq] + (pen + ALIBI[hh] * kposf) for hh in range(H_B)], axis=0)
        _flash_step(s, kv, m_r, a_r)

    expand = (lax.broadcasted_iota(I32, (n_blocks, tq), 0) ==
              (ki * tq + lax.broadcasted_iota(I32, (n_blocks, tq), 1)) // BLK).astype(BF16)
    chosen = jnp.dot(selm_ref[...], expand, preferred_element_type=F32) > 0.5
    branch(selb_ref[0], chosen & (disti >= 0), ms_ref, as_ref)

    @pl.when(ki >= qi - wt)
    def _():
        branch(winb_ref[0], (disti >= 0) & (disti < WINDOW), mw_ref, aw_ref)

    @pl.when(ki == qi)
    def _():
        a_s, a_w = as_ref[...], aw_ref[...]
        o_s = a_s[:, :LANE] / jnp.maximum(a_s[:, LANE:], 1e-30)
        o_w = a_w[:, :LANE] / jnp.maximum(a_w[:, LANE:], 1e-30)
        g = gt_ref[0]
        ys = _merge_heads([o_s[hh * tq:(hh + 1) * tq] for hh in range(H_B)])
        yw = _merge_heads([o_w[hh * tq:(hh + 1) * tq] for hh in range(H_B)])
        y = _head_cols(g, 0) * oc_ref[...] + _head_cols(g, 1) * ys + _head_cols(g, 2) * yw
        o_ref[0] = (y * _group_rsqrt(y, GROUP_W) * gout_ref[:, 256:512]).astype(BF16)


def _nsa_prompt(qn, gt, kvc, selb, winb, lw):
    nb, t, _ = qn.shape
    tq = min(512, t)
    nq = t // tq
    n_blocks = t // BLK
    wt = WINDOW // tq
    qi, ki = _causal_pairs(nq)

    def qmap(b, p, qi, ki):
        return (b, qi[p], 0)

    def kmap(b, p, qi, ki):
        return (b, ki[p], 0)

    def wmap(b, p, qi, ki):
        return (b, jnp.maximum(ki[p], qi[p] - wt), 0)

    grid_spec = pltpu.PrefetchScalarGridSpec(
        num_scalar_prefetch=2,
        grid=(nb, qi.shape[0]),
        in_specs=[pl.BlockSpec((1, tq, 256), qmap), pl.BlockSpec((1, tq, LANE), qmap),
                  pl.BlockSpec((1, n_blocks, LANE), lambda b, p, qi, ki: (b, 0, 0)),
                  pl.BlockSpec((1, tq, 256), kmap), pl.BlockSpec((1, tq, 256), wmap),
                  pl.BlockSpec((1, D_MODEL), lambda b, p, qi, ki: (0, 0))],
        out_specs=pl.BlockSpec((1, tq, 256), qmap),
        scratch_shapes=[pltpu.VMEM((H_B * tq, LANE), BF16), pltpu.VMEM((tq, n_blocks), BF16),
                        pltpu.VMEM((tq, 256), F32),
                        pltpu.VMEM((H_B * tq, LANE), F32), pltpu.VMEM((H_B * tq, 256), F32),
                        pltpu.VMEM((H_B * tq, LANE), F32), pltpu.VMEM((H_B * tq, 256), F32)])
    return pl.pallas_call(
        functools.partial(_nsa_kernel, tq, n_blocks, wt),
        out_shape=jax.ShapeDtypeStruct((nb, t, 256), BF16),
        grid_spec=grid_spec,
        compiler_params=_cparams(("arbitrary", "arbitrary")),
        name="nsa_prompt",
    )(qi, ki, qn, gt, kvc, selb, winb, lw['g_out'])


TOK_ROWS = D_MODEL // LANE


def _store_token_tiles(ref, x):
    n = x.shape[0]
    for j in range(TOK_ROWS):
        ref[pl.ds(j, n, stride=TOK_ROWS), :] = x[:, j * LANE:(j + 1) * LANE]


def _load_token_tiles(ref, n):
    return jnp.concatenate([ref[pl.ds(j, n, stride=TOK_ROWS), :] for j in range(TOK_ROWS)], axis=1)


def _outproj_kernel(ya_ref, yb_ref, yc_ref, yd_ref, x_ref, g1_ref, sc_ref, sh_ref, n2_ref, wo_ref, wr_ref, br_ref,
                    x1_ref, h2_ref, ti_ref, tw_ref):
    mix = jnp.zeros(x_ref.shape, F32)
    for g, y_ref in enumerate((ya_ref, yb_ref, yc_ref, yd_ref)):
        mix = mix + jnp.dot(y_ref[...].astype(BF16), wo_ref[g * GROUP_W:(g + 1) * GROUP_W, :],
                            preferred_element_type=F32)
    x1 = x_ref[...] + g1_ref[0] * mix
    x1_ref[...] = x1
    h2 = x1 * lax.rsqrt(jnp.mean(x1 * x1, -1, keepdims=True) + EPS) * n2_ref[...]
    h2 = h2 * (1.0 + sc_ref[0]) + sh_ref[0]
    _store_token_tiles(h2_ref, h2)
    logits = jnp.dot(h2, wr_ref[...], precision=HI, preferred_element_type=F32) + br_ref[...]
    lane = _lane_iota((1, LANE))
    ti = jnp.zeros(logits.shape, I32)
    tv = jnp.full(logits.shape, NEG_INF, F32)
    for k in range(TOP_K):
        m = jnp.max(logits, -1, keepdims=True)
        idx = jnp.min(jnp.where(logits == m, lane, LANE), -1, keepdims=True)
        ti = jnp.where(lane == k, idx, ti)
        tv = jnp.where(lane == k, m, tv)
        logits = jnp.where(lane == idx, -3e38, logits)
    e = jnp.where(lane < TOP_K, jnp.exp(tv - jnp.max(tv, -1, keepdims=True)), 0.0)
    ti_ref[...] = ti
    tw_ref[...] = e / jnp.sum(e, -1, keepdims=True)


def _outproj(ys, x, g1, sc2, sh2, lw, rows_per_mod):
    n, d = x.shape
    tm = min(512, n)
    per_row = rows_per_mod == 1
    if per_row:
        mod = pl.BlockSpec((1, tm, d), lambda i: (0, i, 0))
        g1, sc2, sh2 = [a.reshape(1, n, d) for a in (g1, sc2, sh2)]
    else:
        mod = pl.BlockSpec((1, 1, d), lambda i: (i * tm // rows_per_mod, 0, 0))
        g1, sc2, sh2 = [a.reshape(-1, 1, d) for a in (g1, sc2, sh2)]

    def row(w):
        return pl.BlockSpec((tm, w), lambda i: (i, 0))

    def const(shape):
        return pl.BlockSpec(shape, lambda i: (0, 0))

    return pl.pallas_call(
        _outproj_kernel,
        out_shape=[jax.ShapeDtypeStruct((n, d), F32), jax.ShapeDtypeStruct((n * TOK_ROWS, LANE), F32),
                   jax.ShapeDtypeStruct((n, LANE), I32), jax.ShapeDtypeStruct((n, LANE), F32)],
        grid=(n // tm,),
        in_specs=[row(256)] * 4 + [row(d), mod, mod, mod, const((1, d)), const((d, d)), const((d, LANE)),
                                    const((1, LANE))],
        out_specs=[row(d), pl.BlockSpec((tm * TOK_ROWS, LANE), lambda i: (i, 0)), row(LANE), row(LANE)],
        compiler_params=_cparams(("arbitrary",)),
        name="outproj",
    )(*ys, x, g1, sc2, sh2, lw['norm2'], lw['w_out'], lw['w_router'], lw['b_router'])


def _start_token_gather(idx_ref, n, src_hbm, buf, sem, straight_line=False):
    def body(j, carry, priority=0):
        r = idx_ref[0, 0, j]
        pltpu.make_async_copy(src_hbm.at[pl.ds(pl.multiple_of(r * TOK_ROWS, TOK_ROWS), TOK_ROWS)],
                              buf.at[pl.ds(pl.multiple_of(j * TOK_ROWS, TOK_ROWS), TOK_ROWS)],
                              sem).start(priority=priority)
        return carry

    if straight_line:
        for j in range(n):
            body(j, 0, priority=j % 2)
    else:
        lax.fori_loop(0, n, body, 0, unroll=8)


def _wait_token_gather(n, src_hbm, buf, sem):
    pltpu.make_async_copy(src_hbm.at[pl.ds(0, n * TOK_ROWS)], buf, sem).wait()


MOE_TM = 512


def _moe_kernel(te_ref, tv_ref, idx0_ref, idxn_ref, h2_hbm, wgu_ref, bgu_ref, wd_ref, bd_ref, o_ref,
                wgu_bf, wd_bf, xbuf, sem):
    i = pl.program_id(0)
    n_tiles = pl.num_programs(0)
    prev = te_ref[jnp.maximum(i - 1, 0)]
    slot = i % 2

    @pl.when((i == 0) & (tv_ref[0] == 1))
    def _():
        _start_token_gather(idx0_ref, MOE_TM, h2_hbm, xbuf.at[0], sem.at[0])

    nxt = jnp.minimum(i + 1, n_tiles - 1)
    nxt_valid = (i + 1 < n_tiles) & (tv_ref[nxt] == 1)

    @pl.when((i == 0) | (te_ref[i] != prev))
    def _():
        for c in range(8):
            wgu_bf[c * 128:(c + 1) * 128, :] = wgu_ref[0, 0, c * 128:(c + 1) * 128, :].astype(BF16)
            wd_bf[c * 128:(c + 1) * 128, :] = wd_ref[0, 0, c * 128:(c + 1) * 128, :].astype(BF16)

    @pl.when(nxt_valid)
    def _():
        _wait_token_gather(MOE_TM, h2_hbm, xbuf.at[slot], sem.at[slot])
        _start_token_gather(idxn_ref, MOE_TM, h2_hbm, xbuf.at[1 - slot], sem.at[1 - slot], straight_line=True)
        _moe_tile(xbuf.at[slot], wgu_bf, bgu_ref, wd_bf, bd_ref, o_ref)

    @pl.when((tv_ref[i] == 1) & jnp.logical_not(nxt_valid))
    def _():
        _wait_token_gather(MOE_TM, h2_hbm, xbuf.at[slot], sem.at[slot])
        _moe_tile(xbuf.at[slot], wgu_bf, bgu_ref, wd_bf, bd_ref, o_ref)

    @pl.when(tv_ref[i] == 0)
    def _():
        o_ref[...] = jnp.zeros_like(o_ref)


def _moe_tile(x_tiles, wgu_bf, bgu_ref, wd_bf, bd_ref, o_ref):
    x = _load_token_tiles(x_tiles, MOE_TM).astype(BF16)
    hu = jnp.dot(x, wgu_bf[...], preferred_element_type=F32) + bgu_ref[0, 0]
    gt = jnp.minimum(hu[:, :D_FF], SWIGLU_LIMIT)
    up = jnp.clip(hu[:, D_FF:], -SWIGLU_LIMIT, SWIGLU_LIMIT)
    act = gt * jax.nn.sigmoid(SWIGLU_ALPHA * gt) * (up + 1.0)
    _store_token_tiles(o_ref, jnp.dot(act.astype(BF16), wd_bf[...], preferred_element_type=F32) + bd_ref[0, 0])


def _moe_ffn(h2_all, tok_of_row, tile_expert, tile_valid, layer, w_gu, b_gu, w_down, b_down):
    d = D_MODEL
    depth = w_gu.shape[0]
    n_tiles = tok_of_row.shape[0] // MOE_TM
    idx = tok_of_row.reshape(n_tiles, 1, MOE_TM)
    grid_spec = pltpu.PrefetchScalarGridSpec(
        num_scalar_prefetch=2,
        grid=(n_tiles,),
        in_specs=[pl.BlockSpec((1, 1, MOE_TM), lambda i, te, tv: (0, 0, 0), memory_space=pltpu.SMEM),
                  pl.BlockSpec((1, 1, MOE_TM), lambda i, te, tv: (jnp.minimum(i + 1, n_tiles - 1), 0, 0),
                               memory_space=pltpu.SMEM),
                  pl.BlockSpec(memory_space=pl.ANY),
                  pl.BlockSpec((1, 1, d, 2 * D_FF), lambda i, te, tv: (layer, te[i], 0, 0)),
                  pl.BlockSpec((1, 1, 1, 2 * D_FF), lambda i, te, tv: (layer, te[i], 0, 0)),
                  pl.BlockSpec((1, 1, D_FF, d), lambda i, te, tv: (layer, te[i], 0, 0)),
                  pl.BlockSpec((1, 1, 1, d), lambda i, te, tv: (layer, te[i], 0, 0))],
        out_specs=pl.BlockSpec((MOE_TM * TOK_ROWS, LANE), lambda i, te, tv: (i, 0)),
        scratch_shapes=[pltpu.VMEM((d, 2 * D_FF), BF16), pltpu.VMEM((D_FF, d), BF16),
                        pltpu.VMEM((2, MOE_TM * TOK_ROWS, LANE), F32), pltpu.SemaphoreType.DMA((2,))])
    return pl.pallas_call(
        _moe_kernel,
        out_shape=jax.ShapeDtypeStruct((n_tiles * MOE_TM * TOK_ROWS, LANE), F32),
        grid_spec=grid_spec,
        compiler_params=_cparams(("arbitrary",)),
        name="moe_ffn",
    )(tile_expert, tile_valid, idx, idx, h2_all, w_gu, b_gu.reshape(depth, N_EXP, 1, 2 * D_FF), w_down,
      b_down.reshape(depth, N_EXP, 1, d))


def _combine_kernel(tm, idx0_ref, idxn_ref, x1_ref, tw_ref, g2_ref, ys_hbm, o_ref, ybuf, sem):
    i = pl.program_id(0)
    n_tiles = pl.num_programs(0)
    slot = i % 2
    n_rows = TOP_K * tm

    @pl.when(i == 0)
    def _():
        _start_token_gather(idx0_ref, n_rows, ys_hbm, ybuf.at[0], sem.at[0])

    @pl.when(i + 1 < n_tiles)
    def _():
        _start_token_gather(idxn_ref, n_rows, ys_hbm, ybuf.at[1 - slot], sem.at[1 - slot], straight_line=True)

    _wait_token_gather(n_rows, ys_hbm, ybuf.at[slot], sem.at[slot])
    tw = tw_ref[...]
    acc = jnp.zeros(x1_ref.shape, F32)
    for k in range(TOP_K):
        yk = _load_token_tiles(ybuf.at[slot, pl.ds(k * tm * TOK_ROWS, tm * TOK_ROWS)], tm)
        acc = acc + tw[:, k:k + 1] * yk
    o_ref[...] = x1_ref[...] + g2_ref[0] * acc


COMBINE_TM = 256


def _combine(x1, ys, rows, tw, g2, rows_per_mod):
    n, d = x1.shape
    tm = min(COMBINE_TM, n)
    n_tiles = n // tm
    idx = jnp.transpose(rows.reshape(n_tiles, tm, TOP_K), (0, 2, 1)).reshape(n_tiles, 1, TOP_K * tm)
    per_row = rows_per_mod == 1
    if per_row:
        mod = pl.BlockSpec((1, tm, d), lambda i: (0, i, 0))
        g2 = g2.reshape(1, n, d)
    else:
        mod = pl.BlockSpec((1, 1, d), lambda i: (i * tm // rows_per_mod, 0, 0))
        g2 = g2.reshape(-1, 1, d)
    return pl.pallas_call(
        functools.partial(_combine_kernel, tm),
        out_shape=jax.ShapeDtypeStruct((n, d), F32),
        grid=(n_tiles,),
        in_specs=[pl.BlockSpec((1, 1, TOP_K * tm), lambda i: (0, 0, 0), memory_space=pltpu.SMEM),
                  pl.BlockSpec((1, 1, TOP_K * tm), lambda i: (jnp.minimum(i + 1, n_tiles - 1), 0, 0),
                               memory_space=pltpu.SMEM),
                  pl.BlockSpec((tm, d), lambda i: (i, 0)), pl.BlockSpec((tm, LANE), lambda i: (i, 0)), mod,
                  pl.BlockSpec(memory_space=pl.ANY)],
        out_specs=pl.BlockSpec((tm, d), lambda i: (i, 0)),
        scratch_shapes=[pltpu.VMEM((2, TOP_K * tm * TOK_ROWS, LANE), F32), pltpu.SemaphoreType.DMA((2,))],
        compiler_params=_cparams(("arbitrary",)),
        name="combine",
    )(idx, idx, x1, tw, g2, ys)


def _moe(h2_all, ti_all, lw_moe):
    n = h2_all.shape[0] // TOK_ROWS
    pairs = n * TOP_K
    e = ti_all[:, :TOP_K].reshape(pairs)
    onehot = (e[:, None] == jnp.arange(N_EXP, dtype=I32)[None, :]).astype(I32)
    csum = jnp.cumsum(onehot, axis=0)
    rank = jnp.take_along_axis(csum, e[:, None], axis=1)[:, 0] - 1
    counts = csum[-1]
    padded = (counts + MOE_TM - 1) // MOE_TM * MOE_TM
    ends = jnp.cumsum(padded)
    starts = ends - padded
    row = starts[e] + rank
    r_max = -(-pairs // MOE_TM) * MOE_TM + N_EXP * MOE_TM
    tok_of_row = jnp.zeros((r_max,), I32).at[row].set(jnp.arange(pairs, dtype=I32) // TOP_K, unique_indices=True)
    tile_start = jnp.arange(r_max // MOE_TM, dtype=I32) * MOE_TM
    tile_valid = (tile_start < ends[-1]).astype(I32)
    tile_expert = jnp.minimum(jnp.sum((tile_start[:, None] >= ends[None, :]).astype(I32), axis=1), N_EXP - 1)
    tile_expert = jnp.where(tile_valid == 1, tile_expert, jnp.max(jnp.where(counts > 0, jnp.arange(N_EXP), 0)))
    ys = _moe_ffn(h2_all, tok_of_row, tile_expert, tile_valid, lw_moe['layer'], lw_moe['w_gu'], lw_moe['b_gu'],
                  lw_moe['w_down'], lw_moe['b_down'])
    return ys, row.reshape(n, TOP_K)


DEC_PAGES = 16
DEC_GROUP = 4


def _rows_to_headmajor(o8, width):
    wide = jnp.concatenate([o8] * H_B, axis=1)
    row = lax.broadcasted_iota(I32, wide.shape, 0)
    lane = lax.broadcasted_iota(I32, wide.shape, 1)
    return jnp.sum(jnp.where(row == lane // width, wide, 0.0), axis=0, keepdims=True)


def _head_rows(q_row, width, pad_to):
    rows = [q_row[:, hh * width:(hh + 1) * width] for hh in range(H_B)]
    q4 = jnp.concatenate(rows + [jnp.zeros((8 - H_B, width), F32)], axis=0)
    if pad_to > width:
        q4 = jnp.concatenate([q4, jnp.zeros((8, pad_to - width), F32)], axis=1)
    return q4


def _slope_rows():
    row = lax.broadcasted_iota(I32, (8, 1), 0)
    out = jnp.zeros((8, 1), F32)
    for hh in range(H_B):
        out = jnp.where(row == hh, ALIBI[hh], out)
    return out


def _nt(a, b, **kw):
    return lax.dot_general(a, b, (((1,), (1,)), ((), ())), preferred_element_type=F32, **kw)


def _decode_stream_kernel(layer, n_pages, pg, pt_ref, qm_ref, kh_ref, cv_ref, gk_ref, wc_ref, wuvw_ref,
                          gout_ref, c_hbm, krt_hbm, cmp_hbm, ya_ref, ksum_ref,
                          cbuf, krbuf, cmpbuf, sem):
    step = pl.program_id(0)
    n_steps = pl.num_programs(0)
    n_chunks = n_pages // pg
    grp = range(DEC_GROUP)

    def copies(st, chunk, slot):
        out = []
        for g in grp:
            for j in range(pg):
                page = pt_ref[st * DEC_GROUP + g, chunk * pg + j]
                rows = pl.ds(j * LANE, LANE)
                out.append(pltpu.make_async_copy(c_hbm.at[layer, page], cbuf.at[g, slot, rows], sem.at[0, slot]))
                out.append(pltpu.make_async_copy(krt_hbm.at[layer, page], krbuf.at[g, slot, j], sem.at[1, slot]))
                out.append(pltpu.make_async_copy(cmp_hbm.at[layer, page], cmpbuf.at[g, slot, rows], sem.at[2, slot]))
        return out

    @pl.when(step == 0)
    def _():
        for cp in copies(0, 0, 0):
            cp.start()

    ones_c = jnp.ones((8, D_LAT), BF16)
    gk = gk_ref[...]
    wc = wc_ref[...]

    init, qs = [], []
    for g in grp:
        q32 = qm_ref[g].astype(F32)
        qlat = jnp.concatenate([q32[:, hh * 256:hh * 256 + D_LAT] for hh in range(H_A)] +
                               [jnp.zeros((8 - H_A, D_LAT), F32)], axis=0)
        qrot = jnp.concatenate([q32[:, hh * 256 + D_LAT + hh * D_ROPE:hh * 256 + D_LAT + (hh + 1) * D_ROPE]
                                for hh in range(H_A)] + [jnp.zeros((8 - H_A, D_ROPE), F32)], axis=0)
        kh_new = kh_ref[g].astype(F32)
        m0 = (jnp.sum(qlat * kh_new[:, :D_LAT], -1, keepdims=True) +
              jnp.sum(qrot * kh_new[:, D_LAT:D_LAT + D_ROPE], -1, keepdims=True))
        init += [m0, jnp.ones((8, 1), F32), jnp.broadcast_to(cv_ref[g][:, :D_LAT].astype(F32), (8, D_LAT))]
        qs.append((qlat.astype(BF16), qrot.astype(BF16)))

    def body(chunk, carry):
        slot = (step * n_chunks + chunk) % 2
        last = chunk + 1 == n_chunks
        nxt_s = jnp.where(last, step + 1, step)
        nxt_c = jnp.where(last, 0, chunk + 1)

        @pl.when(nxt_s < n_steps)
        def _():
            for cp in copies(nxt_s, nxt_c, 1 - slot):
                cp.start()

        for cp in copies(step, chunk, slot):
            cp.wait()
        out = []
        for g in grp:
            m_prev, l_prev, acc = carry[3 * g:3 * g + 3]
            qlat_b, qrot_b = qs[g]
            c = cbuf[g, slot]
            krt = krbuf[g, slot]
            s_rot = jnp.concatenate([jnp.dot(qrot_b, krt[j].astype(BF16), preferred_element_type=F32)
                                     for j in range(pg)], axis=1)
            ss_rot = jnp.concatenate([jnp.sum(krt[j] * krt[j], axis=0, keepdims=True) for j in range(pg)], axis=1)
            ss = _nt(ones_c, (c * c).astype(BF16)) + ss_rot
            r = lax.rsqrt(ss * (1.0 / MLA_DIM) + EPS)
            s = (_nt(qlat_b, (c * gk).astype(BF16)) + s_rot) * r
            m_new = jnp.maximum(m_prev, jnp.max(s, -1, keepdims=True))
            p = jnp.exp(s - m_new)
            alpha = jnp.exp(m_prev - m_new)
            l_new = alpha * l_prev + jnp.sum(p, -1, keepdims=True)
            acc = alpha * acc + jnp.dot(p.astype(BF16), c.astype(BF16), preferred_element_type=F32)
            cm = cmpbuf[g, slot]
            ksum = jnp.sum(cm.reshape(2 * pg, BLK, LANE) * wc[None], axis=1)
            ksum_ref[g, pl.ds(pl.multiple_of(chunk * 2 * pg, 2 * pg), 2 * pg), :] = ksum
            out += [m_new, l_new, acc]
        return tuple(out)

    fin = lax.fori_loop(0, n_chunks, body, tuple(init))
    for g in grp:
        m_f, l_f, acc = fin[3 * g:3 * g + 3]
        o = acc / l_f
        yw = jnp.dot(o.astype(BF16), wuvw_ref[...], preferred_element_type=F32)
        row = lax.broadcasted_iota(I32, yw.shape, 0)
        lane2 = lax.broadcasted_iota(I32, yw.shape, 1)
        ya = jnp.sum(jnp.where(row == lane2 // D_VA, yw, 0.0), axis=0, keepdims=True)
        ya_ref[g] = ya * _group_rsqrt(ya, GROUP_W) * gout_ref[:, 0:256]


def _decode_stream(layer, page_table, qm, kh, cv, lw, cache_c, cache_krt, cache_cmp):
    db, n_pages = page_table.shape
    n_blk = 2 * n_pages
    pg = min(DEC_PAGES, n_pages)

    grp = DEC_GROUP
    assert db % grp == 0

    def per(w):
        return pl.BlockSpec((grp, 1, w), lambda b, pt: (b, 0, 0))

    def const(shape):
        return pl.BlockSpec(shape, lambda b, pt: (0,) * len(shape))

    anyspec = pl.BlockSpec(memory_space=pl.ANY)
    grid_spec = pltpu.PrefetchScalarGridSpec(
        num_scalar_prefetch=1,
        grid=(db // grp,),
        in_specs=[per(1024), per(256), per(256), const((1, LANE)), const((BLK, LANE)),
                  const((LANE, 256)), const((1, D_MODEL)), anyspec, anyspec, anyspec],
        out_specs=[per(256), pl.BlockSpec((grp, n_blk, LANE), lambda b, pt: (b, 0, 0))],
        scratch_shapes=[pltpu.VMEM((grp, 2, pg * LANE, LANE), F32), pltpu.VMEM((grp, 2, pg, D_ROPE, LANE), F32),
                        pltpu.VMEM((grp, 2, pg * LANE, LANE), F32), pltpu.SemaphoreType.DMA((3, 2))])
    ya, ksum = pl.pallas_call(
        functools.partial(_decode_stream_kernel, layer, n_pages, pg),
        out_shape=[jax.ShapeDtypeStruct((db, 1, 256), F32), jax.ShapeDtypeStruct((db, n_blk, LANE), F32)],
        grid_spec=grid_spec,
        compiler_params=_cparams(("arbitrary",)),
        name="decode_stream",
    )(page_table, qm.reshape(db, 1, 1024), kh.reshape(db, 1, 256), cv.reshape(db, 1, 256),
      lw['g_k_mla'], lw['wc'], lw['wuv_wide'], lw['g_out'], cache_c, cache_krt, cache_cmp)
    return ya.reshape(db, 256), ksum


CMP_GROUP = 8


def _cmp_select_kernel(n_blk, past, ksum_ref, qn_ref, gcmp_ref, oc_ref, sel_ref):
    lane = _lane_iota((1, LANE))
    khalf = lane < D_NSA
    jb = lax.broadcasted_iota(I32, (1, n_blk), 1)
    cpos = (jb + 1) * BLK - 1
    bias = _slope_rows() * (past - cpos).astype(F32)
    ocs, imps = [], []
    for i in range(CMP_GROUP):
        ksum = ksum_ref[i]
        ssq = jnp.sum(jnp.where(khalf, ksum * ksum, 0.0), -1, keepdims=True)
        kvc = jnp.where(khalf, ksum * lax.rsqrt(ssq * (1.0 / D_NSA) + EPS) * gcmp_ref[...], ksum)
        q8 = _head_rows(qn_ref[i:i + 1, :], D_NSA, D_NSA)
        s_c = _nt(q8, kvc[:, :D_NSA], precision=HI) - bias
        e_c = jnp.exp(s_c - jnp.max(s_c, -1, keepdims=True))
        p_c = e_c / jnp.maximum(jnp.sum(e_c, -1, keepdims=True), 1e-30)
        oc8 = jnp.dot(p_c.astype(BF16), kvc[:, D_NSA:].astype(BF16), preferred_element_type=F32)
        ocs.append(_rows_to_headmajor(oc8, D_NSA))
        hrow = lax.broadcasted_iota(I32, p_c.shape, 0)
        imps.append(jnp.sum(jnp.where(hrow < H_B, p_c, 0.0), axis=0, keepdims=True))
    oc_ref[...] = jnp.concatenate(ocs, axis=0)
    imp = jnp.where(jb == 0, FORCE_SCORE, jnp.concatenate(imps, axis=0))
    sel = jnp.zeros((CMP_GROUP, LANE), I32)
    for k in range(N_SEL - 1):
        mx = jnp.max(imp, -1, keepdims=True)
        idx = jnp.min(jnp.where(imp == mx, jb, n_blk), -1, keepdims=True)
        sel = jnp.where(lane == k, idx, sel)
        imp = jnp.where(jb == idx, -3e38, imp)
    sel_ref[...] = sel


def _cmp_select(ksum, qn, lw, past):
    db, n_blk, _ = ksum.shape
    return pl.pallas_call(
        functools.partial(_cmp_select_kernel, n_blk, past),
        out_shape=[jax.ShapeDtypeStruct((db, 256), F32), jax.ShapeDtypeStruct((db, LANE), I32)],
        grid=(db // CMP_GROUP,),
        in_specs=[pl.BlockSpec((CMP_GROUP, n_blk, LANE), lambda g: (g, 0, 0)),
                  pl.BlockSpec((CMP_GROUP, 256), lambda g: (g, 0)),
                  pl.BlockSpec((1, LANE), lambda g: (0, 0))],
        out_specs=[pl.BlockSpec((CMP_GROUP, 256), lambda g: (g, 0)),
                   pl.BlockSpec((CMP_GROUP, LANE), lambda g: (g, 0))],
        compiler_params=_cparams(("arbitrary",)),
        name="cmp_select",
    )(ksum, qn, lw['g_k_cmp'])


def _decode_sel_kernel(layer, n_pages, w_buf, pt_ref, si_ref, qn_ref, gt_ref, oc_ref, snew_ref, wnew_ref, win_ref,
                       gout_ref, sel_hbm, yb_ref, selbuf, sem):
    b = pl.program_id(0)
    n_samples = pl.num_programs(0)
    n_sel = N_SEL - 1
    past = n_pages * LANE
    slot = b % 2

    def copies(bb, sl):
        out = []
        for r in range(n_sel):
            j = si_ref[bb, r]
            page = pt_ref[bb, j // 2]
            off = pl.multiple_of((j % 2) * BLK, BLK)
            out.append(pltpu.make_async_copy(sel_hbm.at[layer, page, pl.ds(off, BLK)],
                                             selbuf.at[sl, pl.ds(r * BLK, BLK)], sem.at[sl]))
        return out

    @pl.when(b == 0)
    def _():
        for cp in copies(0, 0):
            cp.start()

    @pl.when(b + 1 < n_samples)
    def _():
        for cp in copies(b + 1, 1 - slot):
            cp.start()

    q8 = _head_rows(qn_ref[0], D_NSA, LANE)
    q8b = q8.astype(BF16)
    q8r = q8b.astype(F32)
    slope = _slope_rows()
    lane = _lane_iota((1, LANE))

    def branch(kv, dist, valid, new_row):
        kvb = kv.astype(BF16)
        s = _nt(q8b, kvb) - slope * dist
        if valid is not None:
            s = jnp.where(valid, s, NEG_INF)
        newb = new_row.astype(BF16).astype(F32)
        s_new = jnp.sum(jnp.where(lane < D_NSA, q8r * newb, 0.0), -1, keepdims=True)
        m = jnp.maximum(jnp.max(s, -1, keepdims=True), s_new)
        e = jnp.exp(s - m)
        if valid is not None:
            e = jnp.where(valid, e, 0.0)
        e_new = jnp.exp(s_new - m)
        den = jnp.sum(e, -1, keepdims=True) + e_new
        o = (jnp.dot(e.astype(BF16), kvb, preferred_element_type=F32) + e_new * newb) / den
        return _rows_to_headmajor(pltpu.roll(o, D_NSA, 1)[:, :D_NSA], D_NSA)

    iw = lax.broadcasted_iota(I32, (1, w_buf), 1)
    distw = w_buf - iw
    validw = (distw < WINDOW) & (past - w_buf + iw >= 0)
    yw = branch(win_ref[0, 0], distw.astype(F32), validw, wnew_ref[0])

    for cp in copies(b, slot):
        cp.wait()
    ls = lax.broadcasted_iota(I32, (1, n_sel * BLK), 1)
    spos = ls % BLK
    for r in range(n_sel):
        spos = spos + jnp.where(ls // BLK == r, si_ref[b, r] * BLK, 0)
    ys = branch(selbuf[slot], (past - spos).astype(F32), None, snew_ref[0])
    g = gt_ref[0]
    y = _head_cols(g, 0) * oc_ref[0] + _head_cols(g, 1) * ys + _head_cols(g, 2) * yw
    yb_ref[0] = y * _group_rsqrt(y, GROUP_W) * gout_ref[:, 256:512]


def _decode_sel(layer, page_table, sel_idx, qn, gt, oc, sel_new, win_new, win_cache, lw, cache_sel):
    db, n_pages = page_table.shape
    w_buf = win_cache.shape[2]

    def per(w):
        return pl.BlockSpec((1, 1, w), lambda b, pt, si: (b, 0, 0))

    grid_spec = pltpu.PrefetchScalarGridSpec(
        num_scalar_prefetch=2,
        grid=(db,),
        in_specs=[per(256), per(LANE), per(256), per(LANE), per(LANE),
                  pl.BlockSpec((1, 1, w_buf, LANE), lambda b, pt, si: (layer, b, 0, 0)),
                  pl.BlockSpec((1, D_MODEL), lambda b, pt, si: (0, 0)),
                  pl.BlockSpec(memory_space=pl.ANY)],
        out_specs=per(256),
        scratch_shapes=[pltpu.VMEM((2, (N_SEL - 1) * BLK, LANE), F32), pltpu.SemaphoreType.DMA((2,))])
    yb = pl.pallas_call(
        functools.partial(_decode_sel_kernel, layer, n_pages, w_buf),
        out_shape=jax.ShapeDtypeStruct((db, 1, 256), F32),
        grid_spec=grid_spec,
        compiler_params=_cparams(("arbitrary",)),
        name="decode_sel",
    )(page_table, sel_idx, qn.reshape(db, 1, 256), gt.reshape(db, 1, LANE), oc.reshape(db, 1, 256),
      sel_new.reshape(db, 1, LANE),
      win_new.reshape(db, 1, LANE), win_cache, lw['g_out'], cache_sel)
    return yb.reshape(db, 256)


def kernel(x_prompt, x_sample, cache_mla_c, cache_mla_kr, cache_nsa_cmp_kv, cache_nsa_sel_kv, cache_nsa_win_kv,
           state_pool, page_table, c_prompt, c_sample, w_ada, b_ada, norm1, norm2, w_in, w_uk, w_uv, g_kv, g_q_mla,
           g_k_mla, g_q_nsa, g_k_cmp, g_k_sel, g_k_win, w_cmp_k, w_cmp_v, w_pool, pool_scale, g_sgu, w_s, b_s,
           g_out, w_out, w_router, b_router, w_gu, b_gu, w_down, b_down):
    w = dict(norm1=norm1, norm2=norm2, w_in=w_in, w_uk=w_uk, w_uv=w_uv, g_kv=g_kv, g_q_mla=g_q_mla, g_k_mla=g_k_mla,
             g_q_nsa=g_q_nsa, g_k_cmp=g_k_cmp, g_k_sel=g_k_sel, g_k_win=g_k_win, w_cmp_k=w_cmp_k, w_cmp_v=w_cmp_v,
             w_pool=w_pool, pool_scale=pool_scale, g_sgu=g_sgu, w_s=w_s, b_s=b_s, g_out=g_out, w_out=w_out,
             w_router=w_router, b_router=b_router, w_gu=w_gu, b_gu=b_gu, w_down=w_down, b_down=b_down)
    nb, t, d = x_prompt.shape
    db = x_sample.shape[0]
    depth = w_ada.shape[0]
    n_tok = nb * t
    past = page_table.shape[1] * cache_mla_c.shape[2]
    w_keep = min(WINDOW, t)

    rows = -(-(nb + db) // 8) * 8
    c_all = jnp.concatenate([c_prompt, c_sample, jnp.zeros((rows - nb - db, d), F32)], axis=0)
    ada = _ada(c_all, w_ada, b_ada)
    cache_krt = jnp.swapaxes(cache_mla_kr, 2, 3)

    xp = x_prompt
    xs = x_sample.reshape(1, db, d)
    st = [[] for _ in range(13)]
    for l in range(depth):
        lw = _layer_weights(l, w)
        sh1p, sc1p, g1p, sh2p, sc2p, g2p = [z[:, None, :] for z in jnp.split(ada[l, :nb], 6, axis=-1)]
        sh1s, sc1s, g1s, sh2s, sc2s, g2s = jnp.split(ada[l, nb:nb + db], 6, axis=-1)

        o = _inproj(xp, sh1p, sc1p, lw, 0)
        ya = _mla_prompt(o['qm'], o['kh'], o['cv'], lw)
        yb = _nsa_prompt(o['qn'], o['gt'], o['kvc'], o['selb'], o['winb'], lw)
        ys = [y.reshape(n_tok, GROUP_W) for y in (ya, yb, o['yc'], o['yd'])]
        x1p, h2p, tip, twp = _outproj(ys, xp.reshape(n_tok, d), g1p, sc2p, sh2p, lw, t)

        s = _inproj(xs, sh1s[None], sc1s[None], lw, past, pool_buf=state_pool[l])
        ya_s, ksum_s = _decode_stream(l, page_table, s['qm'][0], s['kh'][0], s['cv'][0], lw,
                                      cache_mla_c, cache_krt, cache_nsa_cmp_kv)
        oc_s, sel_s = _cmp_select(ksum_s, s['qn'][0], lw, past)
        yb_s = _decode_sel(l, page_table, sel_s, s['qn'][0], s['gt'][0], oc_s, s['sel'][0],
                           s['win'][0], cache_nsa_win_kv, lw, cache_nsa_sel_kv)
        x1s, h2s, tis, tws = _outproj([ya_s, yb_s, s['yc'][0], s['yd'][0]], xs[0], g1s, sc2s, sh2s, lw, 1)

        ys_moe, rows = _moe(jnp.concatenate([h2p, h2s], axis=0), jnp.concatenate([tip, tis], axis=0), lw['moe'])
        xp = _combine(x1p, ys_moe, rows[:n_tok], twp, g2p, t).reshape(nb, t, d)
        xs = _combine(x1s, ys_moe, rows[n_tok:], tws, g2s, 1).reshape(1, db, d)

        per_l = (o['cl'], s['cl'][0][:, None, :], o['kr'], s['kr'][0][:, None, :], o['cmp'], s['cmp'][0][:, None, :],
                 o['sel'], s['sel'][0][:, None, :], o['win'][:, t - w_keep:],
                 jnp.concatenate([cache_nsa_win_kv[l][:, 1:], s['win'][0][:, None, :]], axis=1),
                 o['zc_tail'][:, 16 - POOL_PAD:],
                 jnp.concatenate([state_pool[l][:, 1:], s['zc_tail'][0][:, None, :]], axis=1),
                 s['v_tail'][0][:, None, :])
        for k, v in enumerate(per_l):
            st[k].append(v)
    return (xp, xs.reshape(db, 1, d)) + tuple(jnp.stack(v, axis=0) for v in st)
```
